```python
import math
import jax, jax.numpy as jnp
from jax import lax
import numpy as np

D_MODEL = 1024
BATCH = 32
SEQ = 256
DEPTH = 4
DEC_BATCH = 4
DEC_SEQ = 2048
PAST_LEN = 256

GRID_W = 64
ROPE_BASE = 10000.0
NORM_EPS = 1e-6
N_EVEN = (DEPTH + 1) // 2
N_ODD = DEPTH // 2
D_HALF = D_MODEL // 2
A_HEAD = 64
A_HEADS = D_HALF // A_HEAD
A_DECAY_LORA = 64
A_ICLR_LORA = 64
A_GATE_LORA = 128
GN_EPS = 64e-5
CONV_W = 3
C_HEADS = 8
C_HD = D_MODEL // C_HEADS // 2
Q_BLOCK = 128
P_HEADS = 8
P_NKEYS = 128
P_EXPERTS = P_NKEYS * P_NKEYS
P_QDIM = 256
P_TOPK = 16
P_BLOCK = 128
EVEN_SPLITS = [D_HALF, D_HALF, D_HALF, A_DECAY_LORA, A_ICLR_LORA, A_GATE_LORA, D_HALF, D_HALF, D_HALF]
EVEN_IN = 3 * D_HALF + A_DECAY_LORA + A_ICLR_LORA + A_GATE_LORA + 3 * D_HALF
ODD_IN = 3 * D_MODEL

kernel_name = "hybrid_rwkv7_shortconv_diffattn_peer_dit_step"


def rmsnorm(x, g):
    xf = x.astype(jnp.float32)
    y = xf * lax.rsqrt(jnp.mean(xf * xf, axis=-1, keepdims=True) + NORM_EPS)
    return y.astype(x.dtype) * g


def adaln(cond, w, b):
    m = jax.nn.silu(cond) @ w + b
    return jnp.split(m[:, None, :], 6, axis=-1)


def axial_rope(n_tok, dim, dtype):
    rows = n_tok // GRID_W
    row = jnp.repeat(jnp.arange(rows, dtype=jnp.float32), GRID_W)
    col = jnp.tile(jnp.arange(GRID_W, dtype=jnp.float32), rows)
    nf = dim // 4
    inv = ROPE_BASE ** (-jnp.arange(nf, dtype=jnp.float32) / nf)
    ar = row[:, None] * inv[None, :]
    ac = col[:, None] * inv[None, :]
    ang = jnp.concatenate([ar, ar, ac, ac], axis=-1)
    return jnp.cos(ang).astype(dtype), jnp.sin(ang).astype(dtype)


def apply_rope(x, cos, sin):
    h = x.shape[-1] // 2
    q = h // 2
    xr = jnp.concatenate([-x[..., q:h], x[..., :q], -x[..., h + q:], x[..., h:h + q]], axis=-1)
    return x * cos[None, :, None, None, :] + xr * sin[None, :, None, None, :]


def wkv_scan(s0, r, w, k, v, kk, a, reverse):
    def step(S, inp):
        r_t, w_t, k_t, v_t, kk_t, a_t = inp
        sa = jnp.einsum('bhvk,bhk->bhv', S, -kk_t)
        S = S * w_t[:, :, None, :] + sa[..., None] * (kk_t * a_t)[:, :, None, :] + v_t[..., None] * k_t[:, :, None, :]
        return S, jnp.einsum('bhvk,bhk->bhv', S, r_t)
    xs = tuple(jnp.moveaxis(z, 1, 0) for z in (r, w, k, v, kk, a))
    s_fin, ys = lax.scan(step, s0.astype(jnp.float32), xs, reverse=reverse)
    return s_fin, jnp.moveaxis(ys, 0, 1)


def rwkv7_bidir(r, k, v, wd, ad, gd, s0, w0, w_up, a0, a_up, g_up, k_k, k_a, r_k, ln_g, ln_b):
    B, T, _ = r.shape
    f32 = jnp.float32

    def heads(z):
        return z.astype(f32).reshape(B, T, A_HEADS, A_HEAD)

    kk = heads(k * k_k)
    kk = kk / jnp.maximum(jnp.sqrt(jnp.sum(kk * kk, axis=-1, keepdims=True)), 1e-12)
    rh, vh = heads(r), heads(v)
    tw = jnp.tanh(wd)
    outs, bons, finals = [], [], []
    for d in range(2):
        w_pre = (w0[d] + tw @ w_up[d]).astype(f32)
        decay = heads(jnp.exp(-jnp.exp(-jax.nn.softplus(-w_pre) - 0.5)))
        a = jax.nn.sigmoid(a0[d] + ad @ a_up[d])
        kd = heads(k * (1 + (a - 1) * k_a))
        s_fin, o = wkv_scan(s0[:, d], rh, decay, kd, vh, kk, heads(a), d == 1)
        outs.append(o)
        finals.append(s_fin)
        bons.append(jnp.sum(rh * kd * r_k.astype(f32), axis=-1, keepdims=True) * vh)
    o = outs[0] + outs[1]
    mu = jnp.mean(o, axis=-1, keepdims=True)
    var = jnp.mean(jnp.square(o - mu), axis=-1, keepdims=True)
    o = ((o - mu) * lax.rsqrt(var + GN_EPS)).reshape(B, T, D_HALF) * ln_g + ln_b
    o = o + (bons[0] + bons[1]).reshape(B, T, D_HALF)
    g = jax.nn.sigmoid(gd) @ g_up
    return (o * g).astype(r.dtype), jnp.stack(finals, axis=1)


def short_conv(bg, cg, hc, conv_w):
    z = cg * hc
    T = z.shape[1]
    zp = jnp.pad(z, ((0, 0), (1, 1), (0, 0)))
    y = zp[:, 0:T] * conv_w[0] + zp[:, 1:T + 1] * conv_w[1] + zp[:, 2:T + 2] * conv_w[2]
    return bg * y


def even_mixer(h, s0, w_in, w0, w_up, a0, a_up, g_up, k_k, k_a, r_k, ln_g, ln_b, conv_w, w_out):
    offs = np.cumsum(EVEN_SPLITS[:-1]).tolist()
    r, k, v, wd, ad, gd, bg, cg, hc = jnp.split(h @ w_in, offs, axis=-1)
    y_a, s_fin = rwkv7_bidir(r, k, v, wd, ad, gd, s0, w0, w_up, a0, a_up, g_up, k_k, k_a, r_k, ln_g, ln_b)
    y_b = short_conv(bg, cg, hc, conv_w)
    return jnp.concatenate([y_a, y_b], axis=-1) @ w_out, s_fin


def odd_qkv(h, w_in):
    B, T, _ = h.shape
    q, k, v = jnp.split(h @ w_in, 3, axis=-1)
    return (q.reshape(B, T, C_HEADS, 2, C_HD), k.reshape(B, T, C_HEADS, 2, C_HD),
            v.reshape(B, T, C_HEADS, 2 * C_HD))


def diff_lambda(lp, lam_init):
    lp = lp.astype(jnp.float32)
    return jnp.exp(jnp.sum(lp[0] * lp[1])) - jnp.exp(jnp.sum(lp[2] * lp[3])) + lam_init


def diff_attention(q, k, v, lam, lam_init, subln, w_out):
    B, Tq = q.shape[0], q.shape[1]
    nb = Tq // Q_BLOCK
    qb = jnp.moveaxis(q.reshape(B, nb, Q_BLOCK, C_HEADS, 2, C_HD), 1, 0)
    scale = C_HD ** -0.5

    def blk(qi):
        s = jnp.einsum('bqhjd,bkhjd->bhjqk', qi, k).astype(jnp.float32) * scale
        p = jax.nn.softmax(s, axis=-1)
        attn = p[:, :, 0] - lam * p[:, :, 1]
        return jnp.einsum('bhqk,bkhe->bqhe', attn.astype(v.dtype), v)

    o = jnp.moveaxis(lax.map(blk, qb), 0, 1).reshape(B, Tq, C_HEADS, 2 * C_HD)
    o = rmsnorm(o, subln) * (1.0 - lam_init)
    return o.reshape(B, Tq, D_MODEL) @ w_out


def peer(h, w_q, sub_keys, u_tab, v_tab):
    B, T, D = h.shape
    n = P_BLOCK
    xb = h.reshape(B * T // n, n, D)

    def block(x):
        q = (x @ w_q).reshape(n, P_HEADS, 2, P_QDIM // 2)
        s = jnp.einsum('nhjd,hjkd->nhjk', q, sub_keys).astype(jnp.float32)
        sv, si = lax.top_k(s, P_TOPK)
        cand = (sv[:, :, 0, :, None] + sv[:, :, 1, None, :]).reshape(n, P_HEADS, P_TOPK * P_TOPK)
        cidx = (si[:, :, 0, :, None] * P_NKEYS + si[:, :, 1, None, :]).reshape(n, P_HEADS, P_TOPK * P_TOPK)
        tv, ti = lax.top_k(cand, P_TOPK)
        eidx = jnp.take_along_axis(cidx, ti, axis=-1)
        gate = jax.nn.softmax(tv, axis=-1).astype(x.dtype)
        u = u_tab[eidx]
        act = jax.nn.gelu(jnp.einsum('nhkd,nd->nhk', u, x), approximate=False)
        return jnp.einsum('nhk,nhkd->nd', gate * act, v_tab[eidx])

    return lax.map(block, xb).reshape(B, T, D)


def setup_inputs(seed: int = 0) -> dict:
    key = jax.random.key(seed)
    ks = iter(jax.random.split(key, 40))

    def nrm(shape, s):
        return jax.random.normal(next(ks), shape, jnp.float32) * s

    D = D_MODEL
    return {
        'x_prompt': nrm((BATCH, SEQ, D), 1.0),
        'x_sample': nrm((DEC_BATCH, DEC_SEQ, D), 1.0),
        'c': nrm((DEC_BATCH, D), 1.0),
        'state_rwkv': nrm((DEC_BATCH, N_EVEN, 2, A_HEADS, A_HEAD, A_HEAD), 0.1),
        'cache_k': nrm((DEC_BATCH, N_ODD, PAST_LEN, C_HEADS, 2, C_HD), 1.0),
        'cache_v': nrm((DEC_BATCH, N_ODD, PAST_LEN, C_HEADS, 2 * C_HD), 1.0),
        'c_ctx': nrm((D,), 1.0),
        'w_ada': nrm((DEPTH, D, 6 * D), 0.3 * D ** -0.5),
        'b_ada': nrm((DEPTH, 6 * D), 0.02),
        'norm_mix': 1.0 + nrm((DEPTH, D), 0.05),
        'norm_ffn': 1.0 + nrm((DEPTH, D), 0.05),
        'w_in_even': nrm((N_EVEN, D, EVEN_IN), D ** -0.5),
        'rwkv_w0': nrm((N_EVEN, 2, D_HALF), 0.5),
        'rwkv_w_up': nrm((N_EVEN, 2, A_DECAY_LORA, D_HALF), 0.1 * A_DECAY_LORA ** -0.5),
        'rwkv_a0': nrm((N_EVEN, 2, D_HALF), 0.5),
        'rwkv_a_up': nrm((N_EVEN, 2, A_ICLR_LORA, D_HALF), 0.1 * A_ICLR_LORA ** -0.5),
        'rwkv_g_up': nrm((N_EVEN, A_GATE_LORA, D_HALF), A_GATE_LORA ** -0.5),
        'rwkv_k_k': 0.85 + nrm((N_EVEN, D_HALF), 0.05),
        'rwkv_k_a': 1.0 + nrm((N_EVEN, D_HALF), 0.05),
        'rwkv_r_k': nrm((N_EVEN, A_HEADS, A_HEAD), 0.1),
        'rwkv_ln_g': 1.0 + nrm((N_EVEN, D_HALF), 0.05),
        'rwkv_ln_b': nrm((N_EVEN, D_HALF), 0.02),
        'conv_w': nrm((N_EVEN, CONV_W, D_HALF), CONV_W ** -0.5),
        'w_out_even': nrm((N_EVEN, D, D), D ** -0.5),
        'w_in_odd': nrm((N_ODD, D, ODD_IN), D ** -0.5),
        'diff_lam': nrm((N_ODD, 4, C_HD), 0.1),
        'diff_subln': 1.0 + nrm((N_ODD, 2 * C_HD), 0.05),
        'w_out_odd': nrm((N_ODD, D, D), D ** -0.5),
        'peer_w_q': nrm((DEPTH, D, P_HEADS * P_QDIM), D ** -0.5),
        'peer_sub_keys': nrm((DEPTH, P_HEADS, 2, P_NKEYS, P_QDIM // 2), (P_QDIM // 2) ** -0.5),
        'peer_u': nrm((DEPTH, P_EXPERTS, D), D ** -0.5),
        'peer_v': nrm((DEPTH, P_EXPERTS, D), 0.5),
        'final_norm': 1.0 + nrm((D,), 0.05),
    }


def reference(x_prompt, x_sample, c, state_rwkv, cache_k, cache_v, c_ctx, w_ada, b_ada, norm_mix, norm_ffn,
              w_in_even, rwkv_w0, rwkv_w_up, rwkv_a0, rwkv_a_up, rwkv_g_up, rwkv_k_k, rwkv_k_a, rwkv_r_k,
              rwkv_ln_g, rwkv_ln_b, conv_w, w_out_even, w_in_odd, diff_lam, diff_subln, w_out_odd,
              peer_w_q, peer_sub_keys, peer_u, peer_v, final_norm):
    xp, xs = x_prompt, x_sample
    bp = xp.shape[0]
    cos, sin = axial_rope(xs.shape[1], C_HD, xs.dtype)
    s_zero = jnp.zeros((bp, 2, A_HEADS, A_HEAD, A_HEAD), jnp.float32)
    new_s, new_k, new_v = [], [], []
    for l in range(DEPTH):
        i = l // 2
        sh1p, sc1p, g1p, sh2p, sc2p, g2p = adaln(c_ctx[None, :], w_ada[l], b_ada[l])
        sh1s, sc1s, g1s, sh2s, sc2s, g2s = adaln(c, w_ada[l], b_ada[l])
        hp = rmsnorm(xp, norm_mix[l]) * (1 + sc1p) + sh1p
        hs = rmsnorm(xs, norm_mix[l]) * (1 + sc1s) + sh1s
        if l % 2 == 0:
            ev = (w_in_even[i], rwkv_w0[i], rwkv_w_up[i], rwkv_a0[i], rwkv_a_up[i], rwkv_g_up[i],
                  rwkv_k_k[i], rwkv_k_a[i], rwkv_r_k[i], rwkv_ln_g[i], rwkv_ln_b[i], conv_w[i], w_out_even[i])
            yp, sp = even_mixer(hp, s_zero, *ev)
            ys, _ = even_mixer(hs, state_rwkv[:, i], *ev)
            new_s.append(sp)
        else:
            lam_init = 0.8 - 0.6 * math.exp(-0.3 * l)
            lam = diff_lambda(diff_lam[i], lam_init)
            qp, kp, vp = odd_qkv(hp, w_in_odd[i])
            yp = diff_attention(qp, kp, vp, lam, lam_init, diff_subln[i], w_out_odd[i])
            qs, ks_, vs = odd_qkv(hs, w_in_odd[i])
            k_all = jnp.concatenate([apply_rope(ks_, cos, sin), cache_k[:, i].astype(ks_.dtype)], axis=1)
            v_all = jnp.concatenate([vs, cache_v[:, i].astype(vs.dtype)], axis=1)
            ys = diff_attention(apply_rope(qs, cos, sin), k_all, v_all, lam, lam_init, diff_subln[i], w_out_odd[i])
            new_k.append(kp)
            new_v.append(vp)
        xp = xp + g1p * yp
        xs = xs + g1s * ys
        hp = rmsnorm(xp, norm_ffn[l]) * (1 + sc2p) + sh2p
        hs = rmsnorm(xs, norm_ffn[l]) * (1 + sc2s) + sh2s
        xp = xp + g2p * peer(hp, peer_w_q[l], peer_sub_keys[l], peer_u[l], peer_v[l])
        xs = xs + g2s * peer(hs, peer_w_q[l], peer_sub_keys[l], peer_u[l], peer_v[l])
    y_prompt = rmsnorm(xp, final_norm)
    y_sample = rmsnorm(xs, final_norm)
    new_state_rwkv = jnp.stack(new_s, axis=1)
    new_cache_k = jnp.stack(new_k, axis=1)
    new_cache_v = jnp.stack(new_v, axis=1)
    return (y_prompt, y_sample, new_state_rwkv, new_cache_k, new_cache_v)
```

```python
import functools
import math

import jax
import jax.numpy as jnp
from jax import lax
from jax.experimental import pallas as pl
from jax.experimental.pallas import tpu as pltpu

D_MODEL = 1024
GRID_W = 64
ROPE_BASE = 10000.0
NORM_EPS = 1e-6
D_HALF = D_MODEL // 2
A_HEAD = 64
A_HEADS = D_HALF // A_HEAD
A_DECAY_LORA = 64
A_ICLR_LORA = 64
A_GATE_LORA = 128
GN_EPS = 64e-5
C_HEADS = 8
C_HD = D_MODEL // C_HEADS // 2
P_HEADS = 8
P_NKEYS = 128
P_EXPERTS = P_NKEYS * P_NKEYS
P_QDIM = 256
P_TOPK = 16
LORA_W = A_DECAY_LORA + A_ICLR_LORA + A_GATE_LORA
EVEN_IN = 6 * D_HALF + LORA_W

LANES = 128
SUBLANES = 8
VMEM_LIMIT = 56 * 1024 * 1024

F32 = jnp.float32
BF16 = jnp.bfloat16
NN = (((1,), (0,)), ((), ()))
NT = (((1,), (1,)), ((), ()))
NEG_INF = float("-inf")


def _cparams(*sem):
    return pltpu.CompilerParams(dimension_semantics=sem, vmem_limit_bytes=VMEM_LIMIT)


def _split2(a):
    hi = a.astype(BF16)
    lo = (a - hi.astype(F32)).astype(BF16)
    return hi, lo


def _split3(a):
    hi = a.astype(BF16)
    r = a - hi.astype(F32)
    mid = r.astype(BF16)
    lo = (r - mid.astype(F32)).astype(BF16)
    return hi, mid, lo


def _dot(a, b, dims=NN):
    return lax.dot_general(a, b, dims, preferred_element_type=F32)


def _dot3(a, b_hi, b_lo, dims=NN):
    a_hi, a_lo = _split2(a)
    return _dot(a_hi, b_hi, dims) + (_dot(a_hi, b_lo, dims) + _dot(a_lo, b_hi, dims))


def _dot_exact01(a, ones_bf16):
    hi, mid, lo = _split3(a)
    return _dot(hi, ones_bf16) + (_dot(mid, ones_bf16) + _dot(lo, ones_bf16))


def _mod_row(i, tm, n_prompt, dec_seq):
    np_tiles = n_prompt // tm
    per = dec_seq // tm
    return jnp.where(i < np_tiles, 0, 1 + (i - np_tiles) // per)


def _ada_kernel(c_ref, whi_ref, wlo_ref, b_ref, o_ref):
    c = c_ref[...]
    s = c * jax.nn.sigmoid(c)
    o_ref[0] = _dot3(s, whi_ref[0], wlo_ref[0]) + b_ref[0]


def _adaln_all(cond, w_ada, b_ada):
    depth, d, n6 = w_ada.shape
    rows = cond.shape[0]
    tn = n6 // 4
    whi, wlo = _split2(w_ada)
    return pl.pallas_call(
        _ada_kernel,
        grid=(depth, n6 // tn),
        in_specs=[
            pl.BlockSpec((rows, d), lambda l, j: (0, 0)),
            pl.BlockSpec((1, d, tn), lambda l, j: (l, 0, j)),
            pl.BlockSpec((1, d, tn), lambda l, j: (l, 0, j)),
            pl.BlockSpec((1, 1, tn), lambda l, j: (l, 0, j)),
        ],
        out_specs=pl.BlockSpec((1, rows, tn), lambda l, j: (l, 0, j)),
        out_shape=jax.ShapeDtypeStruct((depth, rows, n6), F32),
        compiler_params=_cparams("parallel", "parallel"),
        name="adaln",
    )(cond, whi, wlo, b_ada.reshape(depth, 1, n6))


def _nmm_kernel(x_ref, g_ref, sc_ref, sh_ref, whi_ref, wlo_ref, o_ref, *maybe_h):
    x = x_ref[...]
    y = x * lax.rsqrt(jnp.mean(x * x, axis=-1, keepdims=True) + NORM_EPS)
    h = y * g_ref[...] * (1.0 + sc_ref[...]) + sh_ref[...]
    o_ref[...] = _dot3(h, whi_ref[...], wlo_ref[...])
    if maybe_h:
        maybe_h[0][...] = h.astype(BF16)


def _norm_mod_matmul(x, gnorm, scale, shift, w, *, tm, n_prompt, dec_seq, emit_h=False, name):
    n, d = x.shape
    nout = w.shape[1]
    tn = nout if emit_h else nout // 2
    whi, wlo = _split2(w)
    mod_map = lambda j, i: (_mod_row(i, tm, n_prompt, dec_seq), 0, 0)
    out_shape = [jax.ShapeDtypeStruct((n, nout), F32)]
    out_specs = [pl.BlockSpec((tm, tn), lambda j, i: (i, j))]
    if emit_h:
        out_shape.append(jax.ShapeDtypeStruct((n, d), BF16))
        out_specs.append(pl.BlockSpec((tm, d), lambda j, i: (i, 0)))
    res = pl.pallas_call(
        _nmm_kernel,
        grid=(nout // tn, n // tm),
        in_specs=[
            pl.BlockSpec((tm, d), lambda j, i: (i, 0)),
            pl.BlockSpec((1, d), lambda j, i: (0, 0)),
            pl.BlockSpec((None, 1, d), mod_map),
            pl.BlockSpec((None, 1, d), mod_map),
            pl.BlockSpec((d, tn), lambda j, i: (0, j)),
            pl.BlockSpec((d, tn), lambda j, i: (0, j)),
        ],
        out_specs=out_specs,
        out_shape=out_shape,
        compiler_params=_cparams("arbitrary", "arbitrary"),
        name=name,
    )(x, gnorm.reshape(1, d), scale, shift, whi, wlo)
    return res if emit_h else res[0]


def _proj_res_kernel(x_ref, a_ref, gate_ref, whi_ref, wlo_ref, o_ref):
    o_ref[...] = x_ref[...] + gate_ref[...] * _dot3(a_ref[...], whi_ref[...], wlo_ref[...])


def _proj_residual(x, a, gate, w, *, tm, n_prompt, dec_seq, name):
    n, d = x.shape
    k = a.shape[1]
    whi, wlo = _split2(w)
    mod_map = lambda i: (_mod_row(i, tm, n_prompt, dec_seq), 0, 0)
    return pl.pallas_call(
        _proj_res_kernel,
        grid=(n // tm,),
        in_specs=[
            pl.BlockSpec((tm, d), lambda i: (i, 0)),
            pl.BlockSpec((tm, k), lambda i: (i, 0)),
            pl.BlockSpec((None, 1, d), mod_map),
            pl.BlockSpec((k, d), lambda i: (0, 0)),
            pl.BlockSpec((k, d), lambda i: (0, 0)),
        ],
        out_specs=pl.BlockSpec((tm, d), lambda i: (i, 0)),
        out_shape=jax.ShapeDtypeStruct((n, d), F32),
        compiler_params=_cparams("parallel"),
        name=name,
    )(x, a, gate, whi, wlo)


def _rwkv_prep_kernel(r_ref, k_ref, v_ref, cg_ref, hc_ref, lora_ref,
                      w0_ref, wuph_ref, wupl_ref, a0_ref, auph_ref, aupl_ref, guph_ref, gupl_ref,
                      kk_ref_p, ka_ref_p, rk_ref_p, ones_ref,
                      kk_out, w_out, kd_out, kka_out, bonus_out, g_out, z_out):
    r = r_ref[...]
    k = k_ref[...]
    v = v_ref[...]
    lora = lora_ref[...]
    ones = ones_ref[...]
    wd = lora[:, :A_DECAY_LORA]
    ad = lora[:, A_DECAY_LORA:A_DECAY_LORA + A_ICLR_LORA]
    gd = lora[:, A_DECAY_LORA + A_ICLR_LORA:]

    kk = k * kk_ref_p[...]
    nrm = jnp.sqrt(_dot_exact01(kk * kk, ones))
    kk = kk / jnp.maximum(nrm, 1e-12)
    kk_out[...] = kk

    tw = jnp.tanh(wd)
    bonus = None
    for d in range(2):
        w_pre = w0_ref[d] + _dot3(tw, wuph_ref[d], wupl_ref[d])
        w_out[d] = jnp.exp(-(math.exp(-0.5) * jax.nn.sigmoid(w_pre)))
        a = jax.nn.sigmoid(a0_ref[d] + _dot3(ad, auph_ref[d], aupl_ref[d]))
        kd = k * (1.0 + (a - 1.0) * ka_ref_p[...])
        kd_out[d] = kd
        kka_out[d] = kk * a
        b = _dot_exact01(r * kd * rk_ref_p[...], ones) * v
        bonus = b if bonus is None else bonus + b
    bonus_out[...] = bonus
    g_out[...] = _dot3(jax.nn.sigmoid(gd), guph_ref[...], gupl_ref[...])
    z_out[...] = cg_ref[...] * hc_ref[...]


def _head_ones():
    idx = jnp.arange(D_HALF) // A_HEAD
    return (idx[:, None] == idx[None, :]).astype(BF16)


def _rwkv_prep(y, w0, w_up, a0, a_up, g_up, k_k, k_a, r_k, *, tm):
    n = y.shape[0]
    h = D_HALF
    wuph, wupl = _split2(w_up)
    auph, aupl = _split2(a_up)
    guph, gupl = _split2(g_up)
    col = lambda c: pl.BlockSpec((tm, h), lambda i: (i, c))
    full2 = lambda s: pl.BlockSpec(s, lambda i: (0, 0))
    full3 = lambda s: pl.BlockSpec(s, lambda i: (0, 0, 0))
    tok = pl.BlockSpec((tm, h), lambda i: (i, 0))
    tok2 = pl.BlockSpec((2, tm, h), lambda i: (0, i, 0))
    sds = lambda *s: jax.ShapeDtypeStruct(s, F32)
    return pl.pallas_call(
        _rwkv_prep_kernel,
        grid=(n // tm,),
        in_specs=[col(0), col(1), col(2), col(4), col(5),
                  pl.BlockSpec((tm, LORA_W), lambda i: (i, 6 * h // LORA_W)),
                  full3((2, 1, h)), full3((2, A_DECAY_LORA, h)), full3((2, A_DECAY_LORA, h)),
                  full3((2, 1, h)), full3((2, A_ICLR_LORA, h)), full3((2, A_ICLR_LORA, h)),
                  full2((A_GATE_LORA, h)), full2((A_GATE_LORA, h)),
                  full2((1, h)), full2((1, h)), full2((1, h)), full2((h, h))],
        out_specs=[tok, tok2, tok2, tok2, tok, tok, tok],
        out_shape=[sds(n, h), sds(2, n, h), sds(2, n, h), sds(2, n, h), sds(n, h), sds(n, h), sds(n, h)],
        compiler_params=_cparams("parallel"),
        name="rwkv_prep",
    )(y, y, y, y, y, y,
      w0.reshape(2, 1, h), wuph, wupl, a0.reshape(2, 1, h), auph, aupl, guph, gupl,
      k_k.reshape(1, h), k_a.reshape(1, h), r_k.reshape(1, h), _head_ones())


def _scan_kernel(r_ref, kk_ref, v_ref, w_ref, kd_ref, kka_ref, s0_ref, o_ref, sf_ref, s_scr, *, tc, g):
    d = pl.program_id(0)
    c = pl.program_id(2)
    nk = A_HEAD
    parts = 4

    @pl.when(c == 0)
    def _():
        s_scr[...] = s0_ref[...]

    def row(ref, t, k):
        return jnp.broadcast_to(ref[t, pl.ds(k, 1), :], (g, LANES))

    def tree(xs):
        while len(xs) > 1:
            xs = [xs[i] + xs[i + 1] for i in range(0, len(xs) - 1, 2)] + ([xs[-1]] if len(xs) % 2 else [])
        return xs[0]

    def step(i, carry):
        t = i + d * (tc - 1 - 2 * i)
        acc = [None] * parts
        for k in range(nk):
            p = s_scr[k] * row(kk_ref, t, k)
            acc[k % parts] = p if acc[k % parts] is None else acc[k % parts] + p
        sa = tree(acc)
        vv = v_ref[t]
        acc = [None] * parts
        for k in range(nk):
            sn = s_scr[k] * row(w_ref, t, k) - sa * row(kka_ref, t, k) + vv * row(kd_ref, t, k)
            s_scr[k] = sn
            p = sn * row(r_ref, t, k)
            acc[k % parts] = p if acc[k % parts] is None else acc[k % parts] + p
        o_ref[t] = tree(acc)
        return carry

    lax.fori_loop(0, tc, step, 0)

    @pl.when(c == pl.num_programs(2) - 1)
    def _():
        sf_ref[...] = s_scr[...]


def _wkv_scan(r, kk, v, w, kd, kka, s0, *, tc):
    lg, t, nk, _ = r.shape
    g = v.shape[2]
    nt = t // tc
    tmap = lambda d, l, c: c + d * (nt - 1 - 2 * c)
    shared = lambda rows: pl.BlockSpec((None, tc, rows, LANES), lambda d, l, c: (l, tmap(d, l, c), 0, 0))
    perdir = pl.BlockSpec((None, None, tc, nk, LANES), lambda d, l, c: (d, l, tmap(d, l, c), 0, 0))
    state = pl.BlockSpec((None, None, nk, g, LANES), lambda d, l, c: (d, l, 0, 0, 0))
    return pl.pallas_call(
        functools.partial(_scan_kernel, tc=tc, g=g),
        grid=(2, lg, nt),
        in_specs=[shared(nk), shared(nk), shared(g), perdir, perdir, perdir, state],
        out_specs=[pl.BlockSpec((None, None, tc, g, LANES), lambda d, l, c: (d, l, tmap(d, l, c), 0, 0)), state],
        out_shape=[jax.ShapeDtypeStruct((2, lg, t, g, LANES), F32),
                   jax.ShapeDtypeStruct((2, lg, nk, g, LANES), F32)],
        scratch_shapes=[pltpu.VMEM((nk, g, LANES), F32)],
        compiler_params=_cparams("parallel", "parallel", "arbitrary"),
        name="wkv_scan",
    )(r, kk, v, w, kd, kka, s0)


def _chain_layout(batch):
    chains = batch * A_HEADS
    if chains >= LANES:
        assert chains % LANES == 0
        return chains // LANES, 1
    assert LANES % chains == 0 and A_HEAD % (LANES // chains) == 0
    return 1, LANES // chains


def _to_lanes_k(x, batch, t, lg, vq):
    x = x.reshape(lg, batch // lg, t, A_HEADS, A_HEAD)
    x = jnp.transpose(x, (0, 2, 4, 1, 3)).reshape(lg, t, A_HEAD, (batch // lg) * A_HEADS)
    return jnp.tile(x, (1, 1, 1, vq)) if vq > 1 else x


def _to_lanes_v(x, batch, t, lg, vq):
    gsz = A_HEAD // vq
    x = x.reshape(lg, batch // lg, t, A_HEADS, vq, gsz)
    x = jnp.transpose(x, (0, 2, 5, 4, 1, 3))
    return x.reshape(lg, t, gsz, vq * (batch // lg) * A_HEADS)


def _from_lanes_v(o, batch, t, lg, vq):
    lead = o.shape[:-4]
    gsz = A_HEAD // vq
    o = o.reshape(lead + (lg, t, gsz, vq, batch // lg, A_HEADS))
    nl = len(lead)
    perm = tuple(range(nl)) + tuple(nl + p for p in (0, 4, 1, 5, 3, 2))
    return jnp.transpose(o, perm).reshape(lead + (batch * t, D_HALF))


def _state_to_lanes(s, lg, vq):
    batch = s.shape[0]
    gsz = A_HEAD // vq
    s = s.reshape(lg, batch // lg, 2, A_HEADS, vq, gsz, A_HEAD)
    s = jnp.transpose(s, (2, 0, 6, 5, 4, 1, 3))
    return s.reshape(2, lg, A_HEAD, gsz, LANES)


def _state_from_lanes(s, batch, lg, vq):
    gsz = A_HEAD // vq
    s = s.reshape(2, lg, A_HEAD, gsz, vq, batch // lg, A_HEADS)
    s = jnp.transpose(s, (1, 5, 0, 6, 4, 3, 2))
    return s.reshape(batch, 2, A_HEADS, A_HEAD, A_HEAD)


def _even_post_kernel(x_ref, o_ref_in, bonus_ref, g_ref, bg_ref, z_ref, zprev_ref, znext_ref,
                      lng_ref, lnb_ref, cw_ref, ones_ref, gate_ref, whi_ref, wlo_ref, out_ref,
                      *, tm, np_tiles, seq_tiles, dec_tiles):
    i = pl.program_id(0)
    ones = ones_ref[...]
    inv = 1.0 / A_HEAD
    o = o_ref_in[0] + o_ref_in[1]
    mu = _dot_exact01(o, ones) * inv
    xc = o - mu
    var = _dot_exact01(xc * xc, ones) * inv
    on = xc * lax.rsqrt(var + GN_EPS) * lng_ref[...] + lnb_ref[...]
    ya = (on + bonus_ref[...]) * g_ref[...]

    pos = jnp.where(i < np_tiles, i % seq_tiles, (i - np_tiles) % dec_tiles)
    last = jnp.where(i < np_tiles, seq_tiles - 1, dec_tiles - 1)
    z = z_ref[...]
    rows = lax.broadcasted_iota(jnp.int32, z.shape, 0)
    prev_row = jnp.where(pos == 0, 0.0, zprev_ref[pl.ds(SUBLANES - 1, 1), :])
    next_row = jnp.where(pos == last, 0.0, znext_ref[pl.ds(0, 1), :])
    zp = jnp.where(rows == 0, prev_row, pltpu.roll(z, 1, axis=0))
    zn = jnp.where(rows == tm - 1, next_row, pltpu.roll(z, tm - 1, axis=0))
    yb = bg_ref[...] * (zp * cw_ref[pl.ds(0, 1), :] + z * cw_ref[pl.ds(1, 1), :] + zn * cw_ref[pl.ds(2, 1), :])

    h = D_HALF
    y = _dot3(ya, whi_ref[pl.ds(0, h), :], wlo_ref[pl.ds(0, h), :]) \
        + _dot3(yb, whi_ref[pl.ds(h, h), :], wlo_ref[pl.ds(h, h), :])
    out_ref[...] = x_ref[...] + gate_ref[...] * y


def _even_post(x, o2, bonus, g, y, z, ln_g, ln_b, conv_w, gate, w_out, *, tm, n_prompt, seq, dec_seq):
    n, d = x.shape
    h = D_HALF
    whi, wlo = _split2(w_out)
    nblk8 = n // SUBLANES
    per8 = tm // SUBLANES
    tok = pl.BlockSpec((tm, h), lambda i: (i, 0))
    full2 = lambda s: pl.BlockSpec(s, lambda i: (0, 0))
    mod_map = lambda i: (_mod_row(i, tm, n_prompt, dec_seq), 0, 0)
    return pl.pallas_call(
        functools.partial(_even_post_kernel, tm=tm, np_tiles=n_prompt // tm, seq_tiles=seq // tm,
                          dec_tiles=dec_seq // tm),
        grid=(n // tm,),
        in_specs=[
            pl.BlockSpec((tm, d), lambda i: (i, 0)),
            pl.BlockSpec((2, tm, h), lambda i: (0, i, 0)),
            tok, tok,
            pl.BlockSpec((tm, h), lambda i: (i, 3)),
            tok,
            pl.BlockSpec((SUBLANES, h), lambda i: (jnp.maximum(i * per8 - 1, 0), 0)),
            pl.BlockSpec((SUBLANES, h), lambda i: (jnp.minimum((i + 1) * per8, nblk8 - 1), 0)),
            full2((1, h)), full2((1, h)), full2((SUBLANES, h)), full2((h, h)),
            pl.BlockSpec((None, 1, d), mod_map),
            full2((d, d)), full2((d, d)),
        ],
        out_specs=pl.BlockSpec((tm, d), lambda i: (i, 0)),
        out_shape=jax.ShapeDtypeStruct((n, d), F32),
        compiler_params=_cparams("parallel"),
        name="even_post",
    )(x, o2, bonus, g, y, z, z, z, ln_g.reshape(1, h), ln_b.reshape(1, h),
      jnp.pad(conv_w, ((0, SUBLANES - conv_w.shape[0]), (0, 0))), _head_ones(), gate, whi, wlo)


def _attn_kernel(lam_ref, q_ref, k_ref, v_ref, subln_ref, o_ref, *, out_scale):
    lam = lam_ref[0]
    hd2 = 2 * C_HD
    for h in range(C_HEADS):
        ps = []
        for j in range(2):
            lo = h * hd2 + j * C_HD
            s = _dot(q_ref[:, lo:lo + C_HD], k_ref[:, lo:lo + C_HD], NT)
            e = jnp.exp(s - jnp.max(s, axis=-1, keepdims=True))
            ps.append((e, 1.0 / jnp.sum(e, axis=-1, keepdims=True)))
        attn = ps[0][0] * ps[0][1] - ps[1][0] * (lam * ps[1][1])
        oh = _dot(attn.astype(BF16), v_ref[:, h * hd2:(h + 1) * hd2])
        y = oh * lax.rsqrt(jnp.mean(oh * oh, axis=-1, keepdims=True) + NORM_EPS)
        o_ref[:, h * hd2:(h + 1) * hd2] = y * subln_ref[...] * out_scale


def _diff_attention(lam, q, k, v, subln, lam_init, *, batch, tq_total, tq):
    d = q.shape[1]
    tk = k.shape[1]
    nq = tq_total // tq
    return pl.pallas_call(
        functools.partial(_attn_kernel, out_scale=1.0 - lam_init),
        grid=(batch, nq),
        in_specs=[
            pl.BlockSpec(memory_space=pltpu.SMEM),
            pl.BlockSpec((tq, d), lambda b, i: (b * nq + i, 0)),
            pl.BlockSpec((None, tk, d), lambda b, i: (b, 0, 0)),
            pl.BlockSpec((None, tk, d), lambda b, i: (b, 0, 0)),
            pl.BlockSpec((1, 2 * C_HD), lambda b, i: (0, 0)),
        ],
        out_specs=pl.BlockSpec((tq, d), lambda b, i: (b * nq + i, 0)),
        out_shape=jax.ShapeDtypeStruct((batch * tq_total, d), F32),
        compiler_params=_cparams("parallel", "parallel"),
        name="diff_attn",
    )(lam.reshape(1), q, k, v, subln.reshape(1, 2 * C_HD))


def _axial_rope_tables(n_tok):
    rows = n_tok // GRID_W
    row = jnp.repeat(jnp.arange(rows, dtype=F32), GRID_W)
    col = jnp.tile(jnp.arange(GRID_W, dtype=F32), rows)
    nf = C_HD // 4
    inv = ROPE_BASE ** (-jnp.arange(nf, dtype=F32) / nf)
    ar = row[:, None] * inv[None, :]
    ac = col[:, None] * inv[None, :]
    ang = jnp.concatenate([ar, ar, ac, ac], axis=-1)
    return jnp.cos(ang), jnp.sin(ang)


def _apply_rope(x, cos, sin):
    h = x.shape[-1] // 2
    q = h // 2
    xr = jnp.concatenate([-x[..., q:h], x[..., :q], -x[..., h + q:], x[..., h:h + q]], axis=-1)
    return x * cos[None, :, None, None, :] + xr * sin[None, :, None, None, :]


N_TOP = P_TOPK + 1
CAND_PAIRS = [(i, j) for i in range(N_TOP) for j in range(N_TOP) if (i + 1) * (j + 1) <= N_TOP]
CAND_ROWS = -(-len(CAND_PAIRS) // SUBLANES) * SUBLANES
TOP_ROWS = -(-N_TOP // SUBLANES) * SUBLANES


def _take_top(x, n, emit):
    for it in range(n):
        m = jnp.max(x, axis=0, keepdims=True)
        emit(it, m)
        if it + 1 < n:
            x = jnp.where(x == m, NEG_INF, x)


def _peer_select_kernel(q_ref, skh_ref, skl_ref, t_out, c1_out, s2_out, e2_out, top_scr, cand_scr):
    half = P_QDIM // 2
    for h in range(P_HEADS):
        sT = []
        for j in range(2):
            hj = 2 * h + j
            qs = q_ref[:, hj * half:(hj + 1) * half]
            q_hi, q_lo = _split2(qs)
            s = _dot(skh_ref[hj], q_hi, NT) + (_dot(skh_ref[hj], q_lo, NT) + _dot(skl_ref[hj], q_hi, NT))
            sT.append(s)

            def emit(it, m, j=j):
                top_scr[j, pl.ds(it, 1), :] = m
            _take_top(s, N_TOP, emit)

        cand_scr[...] = jnp.full(cand_scr.shape, NEG_INF, F32)
        for c, (i, j) in enumerate(CAND_PAIRS):
            cand_scr[pl.ds(c, 1), :] = top_scr[0, pl.ds(i, 1), :] + top_scr[1, pl.ds(j, 1), :]
        a1 = top_scr[0, pl.ds(0, 1), :]
        b1 = top_scr[1, pl.ds(0, 1), :]
        mx = a1 + b1
        st = {"z": None, "c16": None, "c17": None}

        def emit_c(it, m):
            if it < P_TOPK:
                e = jnp.exp(m - mx)
                st["z"] = e if st["z"] is None else st["z"] + e
            if it == P_TOPK - 1:
                st["c16"] = m
            if it == P_TOPK:
                st["c17"] = m
        _take_top(cand_scr[...], N_TOP, emit_c)
        tau = 0.5 * (st["c16"] + st["c17"])
        t_out[h] = tau - sT[0]
        c1_out[h] = jnp.exp(sT[0] - a1) / st["z"]
        s2_out[h] = sT[1]
        e2_out[h] = jnp.exp(sT[1] - b1)


def _peer_select(q, sub_keys, *, tm):
    n = q.shape[0]
    sk = sub_keys.reshape(2 * P_HEADS, P_NKEYS, P_QDIM // 2)
    skh, skl = _split2(sk)
    out = pl.BlockSpec((P_HEADS, P_NKEYS, tm), lambda i: (0, 0, i))
    sds = jax.ShapeDtypeStruct((P_HEADS, P_NKEYS, n), F32)
    return pl.pallas_call(
        _peer_select_kernel,
        grid=(n // tm,),
        in_specs=[pl.BlockSpec((tm, P_HEADS * P_QDIM), lambda i: (i, 0)),
                  pl.BlockSpec(sk.shape, lambda i: (0, 0, 0)),
                  pl.BlockSpec(sk.shape, lambda i: (0, 0, 0))],
        out_specs=[out, out, out, out],
        out_shape=[sds, sds, sds, sds],
        scratch_shapes=[pltpu.VMEM((2, TOP_ROWS, tm), F32), pltpu.VMEM((CAND_ROWS, tm), F32)],
        compiler_params=_cparams("parallel"),
        name="peer_select",
    )(q, skh, skl)


def _peer_dense_kernel(hb_ref, x_ref, gate_ref, u_ref, vt_ref, t_ref, c1_ref, s2_ref, e2_ref, o_ref,
                       acc_ref, a_scr, *, i1_per_step, tm):
    c = pl.program_id(1)

    @pl.when(c == 0)
    def _():
        acc_ref[...] = jnp.zeros(acc_ref.shape, F32)

    sc = _dot(u_ref[...], hb_ref[...], NT)
    act = 0.5 * sc * (1.0 + lax.erf(sc * (1.0 / math.sqrt(2.0))))
    for i1 in range(i1_per_step):
        for lg in range(tm // LANES):
            ls = slice(lg * LANES, (lg + 1) * LANES)
            w = None
            for h in range(P_HEADS):
                t = t_ref[h, pl.ds(i1, 1), ls]
                c1 = c1_ref[h, pl.ds(i1, 1), ls]
                term = jnp.where(s2_ref[h, :, ls] >= t, e2_ref[h, :, ls] * c1, 0.0)
                w = term if w is None else w + term
            rs = slice(i1 * P_NKEYS, (i1 + 1) * P_NKEYS)
            a_scr[rs, ls] = (w * act[rs, ls]).astype(BF16)
    acc_ref[...] += _dot(vt_ref[...], a_scr[...])

    @pl.when(c == pl.num_programs(1) - 1)
    def _():
        o_ref[...] = x_ref[...] + gate_ref[...] * acc_ref[...].T


def _peer_dense(x, hb, gate, u_bf, vt_bf, t, c1, s2, e2, *, tm, i1_per_step, n_prompt, dec_seq):
    n, d = x.shape
    ec = i1_per_step * P_NKEYS
    mod_map = lambda i, c: (_mod_row(i, tm, n_prompt, dec_seq), 0, 0)
    small = pl.BlockSpec((P_HEADS, i1_per_step, tm), lambda i, c: (0, c, i))
    big = pl.BlockSpec((P_HEADS, P_NKEYS, tm), lambda i, c: (0, 0, i))
    return pl.pallas_call(
        functools.partial(_peer_dense_kernel, i1_per_step=i1_per_step, tm=tm),
        grid=(n // tm, P_EXPERTS // ec),
        in_specs=[
            pl.BlockSpec((tm, d), lambda i, c: (i, 0)),
            pl.BlockSpec((tm, d), lambda i, c: (i, 0)),
            pl.BlockSpec((None, 1, d), mod_map),
            pl.BlockSpec((ec, d), lambda i, c: (c, 0)),
            pl.BlockSpec((d, ec), lambda i, c: (0, c)),
            small, small, big, big,
        ],
        out_specs=pl.BlockSpec((tm, d), lambda i, c: (i, 0)),
        out_shape=jax.ShapeDtypeStruct((n, d), F32),
        scratch_shapes=[pltpu.VMEM((d, tm), F32), pltpu.VMEM((ec, tm), BF16)],
        compiler_params=_cparams("parallel", "arbitrary"),
        name="peer_dense",
    )(hb, x, gate, u_bf, vt_bf, t, c1, s2, e2)


def _final_norm_kernel(x_ref, g_ref, o_ref):
    x = x_ref[...]
    o_ref[...] = x * lax.rsqrt(jnp.mean(x * x, axis=-1, keepdims=True) + NORM_EPS) * g_ref[...]


def _final_norm(x, g, *, tm):
    n, d = x.shape
    return pl.pallas_call(
        _final_norm_kernel,
        grid=(n // tm,),
        in_specs=[pl.BlockSpec((tm, d), lambda i: (i, 0)), pl.BlockSpec((1, d), lambda i: (0, 0))],
        out_specs=pl.BlockSpec((tm, d), lambda i: (i, 0)),
        out_shape=jax.ShapeDtypeStruct((n, d), F32),
        compiler_params=_cparams("parallel"),
        name="final_norm",
    )(x, g.reshape(1, d))


def _tiles(seq, dec_seq):
    tm = math.gcd(math.gcd(seq, dec_seq), 256)
    tm_dense = math.gcd(math.gcd(seq, dec_seq), 512)
    return tm, tm_dense


def kernel(x_prompt, x_sample, c, state_rwkv, cache_k, cache_v, c_ctx, w_ada, b_ada, norm_mix, norm_ffn, w_in_even, rwkv_w0, rwkv_w_up, rwkv_a0, rwkv_a_up, rwkv_g_up, rwkv_k_k, rwkv_k_a, rwkv_r_k, rwkv_ln_g, rwkv_ln_b, conv_w, w_out_even, w_in_odd, diff_lam, diff_subln, w_out_odd, peer_w_q, peer_sub_keys, peer_u, peer_v, final_norm):
    bp, seq, d = x_prompt.shape
    bs, dec_seq, _ = x_sample.shape
    depth = w_ada.shape[0]
    n_prompt = bp * seq
    n_sample = bs * dec_seq
    tm, tm_dense = _tiles(seq, dec_seq)
    tm_dense = math.gcd(math.gcd(n_prompt, dec_seq), 512)
    tok = dict(n_prompt=n_prompt, dec_seq=dec_seq)

    x = jnp.concatenate([x_prompt.reshape(n_prompt, d), x_sample.reshape(n_sample, d)], axis=0)

    rows = 1 + bs
    rows_pad = -(-rows // SUBLANES) * SUBLANES
    cond = jnp.zeros((rows_pad, d), F32).at[0].set(c_ctx).at[1:rows].set(c)
    mod = _adaln_all(cond, w_ada, b_ada)[:, :rows].reshape(depth, rows, 6, 1, d)
    mods = [[mod[l, :, m] for m in range(6)] for l in range(depth)]

    h = D_HALF
    perm = jnp.concatenate([jnp.arange(0, 3 * h), jnp.arange(3 * h + LORA_W, 6 * h + LORA_W),
                            jnp.arange(3 * h, 3 * h + LORA_W)])

    lg_p, vq_p = _chain_layout(bp)
    lg_s, vq_s = _chain_layout(bs)
    cos, sin = _axial_rope_tables(dec_seq)
    scale = C_HD ** -0.5

    new_s, new_k, new_v = [], [], []
    for l in range(depth):
        i = l // 2
        sh1, sc1, g1, sh2, sc2, g2 = mods[l]
        if l % 2 == 0:
            y = _norm_mod_matmul(x, norm_mix[l], sc1, sh1, w_in_even[i][:, perm], tm=tm, name="in_even", **tok)
            kk, w, kd, kka, bonus, g, z = _rwkv_prep(
                y, rwkv_w0[i], rwkv_w_up[i], rwkv_a0[i], rwkv_a_up[i], rwkv_g_up[i],
                rwkv_k_k[i], rwkv_k_a[i], rwkv_r_k[i].reshape(-1), tm=tm)
            r = y[:, 0:h]
            v = y[:, 2 * h:3 * h]
            outs = []
            for (lo, n_tok, batch, t, lg, vq, s0) in (
                    (0, n_prompt, bp, seq, lg_p, vq_p, None),
                    (n_prompt, n_sample, bs, dec_seq, lg_s, vq_s, state_rwkv[:, i])):
                sl = slice(lo, lo + n_tok)
                lk = lambda a: _to_lanes_k(a, batch, t, lg, vq)
                gsz = A_HEAD // vq
                s0l = (jnp.zeros((2, lg, A_HEAD, gsz, LANES), F32) if s0 is None
                       else _state_to_lanes(s0, lg, vq))
                o, sf = _wkv_scan(lk(r[sl]), lk(kk[sl]), _to_lanes_v(v[sl], batch, t, lg, vq),
                                  jnp.stack([lk(w[dd, sl]) for dd in range(2)]),
                                  jnp.stack([lk(kd[dd, sl]) for dd in range(2)]),
                                  jnp.stack([lk(kka[dd, sl]) for dd in range(2)]),
                                  s0l, tc=min(16, t))
                outs.append(_from_lanes_v(o, batch, t, lg, vq))
                if s0 is None:
                    new_s.append(_state_from_lanes(sf, batch, lg, vq))
            o2 = jnp.concatenate(outs, axis=1)
            x = _even_post(x, o2, bonus, g, y, z, rwkv_ln_g[i], rwkv_ln_b[i], conv_w[i], g1, w_out_even[i],
                           tm=tm, n_prompt=n_prompt, seq=seq, dec_seq=dec_seq)
        else:
            lam_init = 0.8 - 0.6 * math.exp(-0.3 * l)
            lp = diff_lam[i].astype(F32)
            lam = jnp.exp(jnp.sum(lp[0] * lp[1])) - jnp.exp(jnp.sum(lp[2] * lp[3])) + lam_init
            y = _norm_mod_matmul(x, norm_mix[l], sc1, sh1, w_in_odd[i], tm=tm, name="in_odd", **tok)
            q, k, v = y[:, 0:d], y[:, d:2 * d], y[:, 2 * d:3 * d]
            new_k.append(k[:n_prompt].reshape(bp, seq, C_HEADS, 2, C_HD))
            new_v.append(v[:n_prompt].reshape(bp, seq, C_HEADS, 2 * C_HD))
            o_p = _diff_attention(lam, (q[:n_prompt] * scale).astype(BF16),
                                  k[:n_prompt].reshape(bp, seq, d).astype(BF16),
                                  v[:n_prompt].reshape(bp, seq, d).astype(BF16),
                                  diff_subln[i], lam_init, batch=bp, tq_total=seq, tq=min(seq, 256))
            shp = (bs, dec_seq, C_HEADS, 2, C_HD)
            qs = _apply_rope(q[n_prompt:].reshape(shp), cos, sin).reshape(n_sample, d)
            ks = _apply_rope(k[n_prompt:].reshape(shp), cos, sin).reshape(bs, dec_seq, d)
            k_all = jnp.concatenate([ks, cache_k[:, i].reshape(bs, -1, d)], axis=1).astype(BF16)
            v_all = jnp.concatenate([v[n_prompt:].reshape(bs, dec_seq, d), cache_v[:, i].reshape(bs, -1, d)],
                                    axis=1).astype(BF16)
            o_s = _diff_attention(lam, (qs * scale).astype(BF16), k_all, v_all, diff_subln[i], lam_init,
                                  batch=bs, tq_total=dec_seq, tq=min(dec_seq, 256))
            x = _proj_residual(x, jnp.concatenate([o_p, o_s], axis=0), g1, w_out_odd[i], tm=tm,
                               name="out_odd", **tok)

        qp, hb = _norm_mod_matmul(x, norm_ffn[l], sc2, sh2, peer_w_q[l], tm=tm, emit_h=True, name="peer_q", **tok)
        t_, c1_, s2_, e2_ = _peer_select(qp, peer_sub_keys[l], tm=tm)
        x = _peer_dense(x, hb, g2, peer_u[l].astype(BF16), peer_v[l].T.astype(BF16), t_, c1_, s2_, e2_,
                        tm=tm_dense, i1_per_step=8, **tok)

    yfin = _final_norm(x, final_norm, tm=tm)
    y_prompt = yfin[:n_prompt].reshape(bp, seq, d)
    y_sample = yfin[n_prompt:].reshape(bs, dec_seq, d)
    return (y_prompt, y_sample, jnp.stack(new_s, axis=1), jnp.stack(new_k, axis=1), jnp.stack(new_v, axis=1))
```

```python
import functools
import math

import jax
import jax.numpy as jnp
from jax import lax
from jax.experimental import pallas as pl
from jax.experimental.pallas import tpu as pltpu

D_MODEL = 1024
GRID_W = 64
ROPE_BASE = 10000.0
NORM_EPS = 1e-6
D_HALF = D_MODEL // 2
A_HEAD = 64
A_HEADS = D_HALF // A_HEAD
A_DECAY_LORA = 64
A_ICLR_LORA = 64
A_GATE_LORA = 128
GN_EPS = 64e-5
C_HEADS = 8
C_HD = D_MODEL // C_HEADS // 2
P_HEADS = 8
P_NKEYS = 128
P_EXPERTS = P_NKEYS * P_NKEYS
P_QDIM = 256
P_TOPK = 16
LORA_W = A_DECAY_LORA + A_ICLR_LORA + A_GATE_LORA
EVEN_IN = 6 * D_HALF + LORA_W

LANES = 128
SUBLANES = 8
VMEM_LIMIT = 56 * 1024 * 1024

F32 = jnp.float32
BF16 = jnp.bfloat16
NN = (((1,), (0,)), ((), ()))
NT = (((1,), (1,)), ((), ()))
NEG_INF = float("-inf")


def _cparams(*sem):
    return pltpu.CompilerParams(dimension_semantics=sem, vmem_limit_bytes=VMEM_LIMIT)


def _split2(a):
    hi = a.astype(BF16)
    lo = (a - hi.astype(F32)).astype(BF16)
    return hi, lo


def _split3(a):
    hi = a.astype(BF16)
    r = a - hi.astype(F32)
    mid = r.astype(BF16)
    lo = (r - mid.astype(F32)).astype(BF16)
    return hi, mid, lo


def _dot(a, b, dims=NN):
    return lax.dot_general(a, b, dims, preferred_element_type=F32)


def _dot3(a, b_hi, b_lo, dims=NN):
    a_hi, a_lo = _split2(a)
    return _dot(a_hi, b_hi, dims) + (_dot(a_hi, b_lo, dims) + _dot(a_lo, b_hi, dims))


def _wparts(w, passes):
    return (w.astype(BF16),) if passes == 1 else _split2(w)


def _wdot(a, parts, dims=NN):
    if len(parts) == 1:
        return _dot(a.astype(BF16), parts[0], dims)
    return _dot3(a, parts[0], parts[1], dims)


def _dot_exact01(a, ones_bf16):
    hi, mid, lo = _split3(a)
    return _dot(hi, ones_bf16) + (_dot(mid, ones_bf16) + _dot(lo, ones_bf16))


def _mod_row(i, tm, n_prompt, dec_seq):
    np_tiles = n_prompt // tm
    per = dec_seq // tm
    return jnp.where(i < np_tiles, 0, 1 + (i - np_tiles) // per)


def _ada_kernel(c_ref, whi_ref, wlo_ref, b_ref, o_ref):
    c = c_ref[...]
    s = c * jax.nn.sigmoid(c)
    o_ref[0] = _dot3(s, whi_ref[0], wlo_ref[0]) + b_ref[0]


def _adaln_all(cond, w_ada, b_ada):
    depth, d, n6 = w_ada.shape
    rows = cond.shape[0]
    tn = n6 // 4
    whi, wlo = _split2(w_ada)
    return pl.pallas_call(
        _ada_kernel,
        grid=(depth, n6 // tn),
        in_specs=[
            pl.BlockSpec((rows, d), lambda l, j: (0, 0)),
            pl.BlockSpec((1, d, tn), lambda l, j: (l, 0, j)),
            pl.BlockSpec((1, d, tn), lambda l, j: (l, 0, j)),
            pl.BlockSpec((1, 1, tn), lambda l, j: (l, 0, j)),
        ],
        out_specs=pl.BlockSpec((1, rows, tn), lambda l, j: (l, 0, j)),
        out_shape=jax.ShapeDtypeStruct((depth, rows, n6), F32),
        compiler_params=_cparams("parallel", "parallel"),
        name="adaln",
    )(cond, whi, wlo, b_ada.reshape(depth, 1, n6))


def _nmm_kernel(x_ref, g_ref, sc_ref, sh_ref, *rest, n_w):
    w_refs, o_ref, maybe_h = rest[:n_w], rest[n_w], rest[n_w + 1:]
    x = x_ref[...]
    y = x * lax.rsqrt(jnp.mean(x * x, axis=-1, keepdims=True) + NORM_EPS)
    h = y * g_ref[...] * (1.0 + sc_ref[...]) + sh_ref[...]
    o_ref[...] = _wdot(h, tuple(r[...] for r in w_refs))
    if maybe_h:
        maybe_h[0][...] = h.astype(BF16)


def _norm_mod_matmul(x, gnorm, scale, shift, w, *, tm, n_prompt, dec_seq, passes, emit_h=False, name):
    n, d = x.shape
    nout = w.shape[1]
    tn = nout if emit_h else nout // 2
    wp = _wparts(w, passes)
    mod_map = lambda j, i: (_mod_row(i, tm, n_prompt, dec_seq), 0, 0)
    out_shape = [jax.ShapeDtypeStruct((n, nout), F32)]
    out_specs = [pl.BlockSpec((tm, tn), lambda j, i: (i, j))]
    if emit_h:
        out_shape.append(jax.ShapeDtypeStruct((n, d), BF16))
        out_specs.append(pl.BlockSpec((tm, d), lambda j, i: (i, 0)))
    res = pl.pallas_call(
        functools.partial(_nmm_kernel, n_w=len(wp)),
        grid=(nout // tn, n // tm),
        in_specs=[
            pl.BlockSpec((tm, d), lambda j, i: (i, 0)),
            pl.BlockSpec((1, d), lambda j, i: (0, 0)),
            pl.BlockSpec((None, 1, d), mod_map),
            pl.BlockSpec((None, 1, d), mod_map),
        ] + [pl.BlockSpec((d, tn), lambda j, i: (0, j))] * len(wp),
        out_specs=out_specs,
        out_shape=out_shape,
        compiler_params=_cparams("arbitrary", "arbitrary"),
        name=name,
    )(x, gnorm.reshape(1, d), scale, shift, *wp)
    return res if emit_h else res[0]


def _proj_res_kernel(x_ref, a_ref, gate_ref, w_ref, o_ref):
    o_ref[...] = x_ref[...] + gate_ref[...] * _dot(a_ref[...].astype(BF16), w_ref[...])


def _proj_residual(x, a, gate, w, *, tm, n_prompt, dec_seq, name):
    n, d = x.shape
    k = a.shape[1]
    mod_map = lambda i: (_mod_row(i, tm, n_prompt, dec_seq), 0, 0)
    return pl.pallas_call(
        _proj_res_kernel,
        grid=(n // tm,),
        in_specs=[
            pl.BlockSpec((tm, d), lambda i: (i, 0)),
            pl.BlockSpec((tm, k), lambda i: (i, 0)),
            pl.BlockSpec((None, 1, d), mod_map),
            pl.BlockSpec((k, d), lambda i: (0, 0)),
        ],
        out_specs=pl.BlockSpec((tm, d), lambda i: (i, 0)),
        out_shape=jax.ShapeDtypeStruct((n, d), F32),
        compiler_params=_cparams("parallel"),
        name=name,
    )(x, a, gate, w.astype(BF16))


def _rwkv_prep_kernel(r_ref, k_ref, v_ref, cg_ref, hc_ref, lora_ref,
                      w0_ref, wuph_ref, wupl_ref, a0_ref, auph_ref, aupl_ref, guph_ref, gupl_ref,
                      kk_ref_p, ka_ref_p, rk_ref_p, ones_ref,
                      kk_out, w_out, kd_out, kka_out, bonus_out, g_out, z_out):
    r = r_ref[...]
    k = k_ref[...]
    v = v_ref[...]
    lora = lora_ref[...]
    ones = ones_ref[...]
    wd = lora[:, :A_DECAY_LORA]
    ad = lora[:, A_DECAY_LORA:A_DECAY_LORA + A_ICLR_LORA]
    gd = lora[:, A_DECAY_LORA + A_ICLR_LORA:]

    kk = k * kk_ref_p[...]
    nrm = jnp.sqrt(_dot_exact01(kk * kk, ones))
    kk = kk / jnp.maximum(nrm, 1e-12)
    kk_out[...] = kk

    tw = jnp.tanh(wd)
    bonus = None
    for d in range(2):
        w_pre = w0_ref[d] + _dot3(tw, wuph_ref[d], wupl_ref[d])
        w_out[d] = jnp.exp(-(math.exp(-0.5) * jax.nn.sigmoid(w_pre)))
        a = jax.nn.sigmoid(a0_ref[d] + _dot3(ad, auph_ref[d], aupl_ref[d]))
        kd = k * (1.0 + (a - 1.0) * ka_ref_p[...])
        kd_out[d] = kd
        kka_out[d] = kk * a
        b = _dot_exact01(r * kd * rk_ref_p[...], ones) * v
        bonus = b if bonus is None else bonus + b
    bonus_out[...] = bonus
    g_out[...] = _dot3(jax.nn.sigmoid(gd), guph_ref[...], gupl_ref[...])
    z_out[...] = cg_ref[...] * hc_ref[...]


def _head_ones():
    idx = jnp.arange(D_HALF) // A_HEAD
    return (idx[:, None] == idx[None, :]).astype(BF16)


def _rwkv_prep(y, w0, w_up, a0, a_up, g_up, k_k, k_a, r_k, *, tm):
    n = y.shape[0]
    h = D_HALF
    wuph, wupl = _split2(w_up)
    auph, aupl = _split2(a_up)
    guph, gupl = _split2(g_up)
    col = lambda c: pl.BlockSpec((tm, h), lambda i: (i, c))
    full2 = lambda s: pl.BlockSpec(s, lambda i: (0, 0))
    full3 = lambda s: pl.BlockSpec(s, lambda i: (0, 0, 0))
    tok = pl.BlockSpec((tm, h), lambda i: (i, 0))
    tok2 = pl.BlockSpec((2, tm, h), lambda i: (0, i, 0))
    sds = lambda *s: jax.ShapeDtypeStruct(s, F32)
    return pl.pallas_call(
        _rwkv_prep_kernel,
        grid=(n // tm,),
        in_specs=[col(0), col(1), col(2), col(4), col(5),
                  pl.BlockSpec((tm, LORA_W), lambda i: (i, 6 * h // LORA_W)),
                  full3((2, 1, h)), full3((2, A_DECAY_LORA, h)), full3((2, A_DECAY_LORA, h)),
                  full3((2, 1, h)), full3((2, A_ICLR_LORA, h)), full3((2, A_ICLR_LORA, h)),
                  full2((A_GATE_LORA, h)), full2((A_GATE_LORA, h)),
                  full2((1, h)), full2((1, h)), full2((1, h)), full2((h, h))],
        out_specs=[tok, tok2, tok2, tok2, tok, tok, tok],
        out_shape=[sds(n, h), sds(2, n, h), sds(2, n, h), sds(2, n, h), sds(n, h), sds(n, h), sds(n, h)],
        compiler_params=_cparams("parallel"),
        name="rwkv_prep",
    )(y, y, y, y, y, y,
      w0.reshape(2, 1, h), wuph, wupl, a0.reshape(2, 1, h), auph, aupl, guph, gupl,
      k_k.reshape(1, h), k_a.reshape(1, h), r_k.reshape(1, h), _head_ones())


def _scan_kernel(r_ref, kk_ref, v_ref, w_ref, kd_ref, kka_ref, s0_ref, o_ref, sf_ref, s_scr, *, tc, g):
    d = pl.program_id(0)
    c = pl.program_id(2)
    nk = A_HEAD
    parts = 4

    @pl.when(c == 0)
    def _():
        s_scr[...] = s0_ref[...]

    def row(ref, t, k):
        return jnp.broadcast_to(ref[t, pl.ds(k, 1), :], (g, LANES))

    def tree(xs):
        while len(xs) > 1:
            xs = [xs[i] + xs[i + 1] for i in range(0, len(xs) - 1, 2)] + ([xs[-1]] if len(xs) % 2 else [])
        return xs[0]

    def step(i, carry):
        t = i + d * (tc - 1 - 2 * i)
        acc = [None] * parts
        for k in range(nk):
            p = s_scr[k] * row(kk_ref, t, k)
            acc[k % parts] = p if acc[k % parts] is None else acc[k % parts] + p
        sa = tree(acc)
        vv = v_ref[t]
        acc = [None] * parts
        for k in range(nk):
            sn = s_scr[k] * row(w_ref, t, k) - sa * row(kka_ref, t, k) + vv * row(kd_ref, t, k)
            s_scr[k] = sn
            p = sn * row(r_ref, t, k)
            acc[k % parts] = p if acc[k % parts] is None else acc[k % parts] + p
        o_ref[t] = tree(acc)
        return carry

    lax.fori_loop(0, tc, step, 0)

    @pl.when(c == pl.num_programs(2) - 1)
    def _():
        sf_ref[...] = s_scr[...]


def _wkv_scan(r, kk, v, w, kd, kka, s0, *, tc):
    lg, t, nk, _ = r.shape
    g = v.shape[2]
    nt = t // tc
    tmap = lambda d, l, c: c + d * (nt - 1 - 2 * c)
    shared = lambda rows: pl.BlockSpec((None, tc, rows, LANES), lambda d, l, c: (l, tmap(d, l, c), 0, 0))
    perdir = pl.BlockSpec((None, None, tc, nk, LANES), lambda d, l, c: (d, l, tmap(d, l, c), 0, 0))
    state = pl.BlockSpec((None, None, nk, g, LANES), lambda d, l, c: (d, l, 0, 0, 0))
    return pl.pallas_call(
        functools.partial(_scan_kernel, tc=tc, g=g),
        grid=(2, lg, nt),
        in_specs=[shared(nk), shared(nk), shared(g), perdir, perdir, perdir, state],
        out_specs=[pl.BlockSpec((None, None, tc, g, LANES), lambda d, l, c: (d, l, tmap(d, l, c), 0, 0)), state],
        out_shape=[jax.ShapeDtypeStruct((2, lg, t, g, LANES), F32),
                   jax.ShapeDtypeStruct((2, lg, nk, g, LANES), F32)],
        scratch_shapes=[pltpu.VMEM((nk, g, LANES), F32)],
        compiler_params=_cparams("parallel", "parallel", "arbitrary"),
        name="wkv_scan",
    )(r, kk, v, w, kd, kka, s0)


def _chain_layout(batch):
    chains = batch * A_HEADS
    if chains >= LANES:
        assert chains % LANES == 0
        return chains // LANES, 1
    assert LANES % chains == 0 and A_HEAD % (LANES // chains) == 0
    return 1, LANES // chains


def _to_lanes_k(x, batch, t, lg, vq):
    lead = x.shape[:-2]
    nl = len(lead)
    chains = (batch // lg) * A_HEADS
    x = x.reshape(lead + (lg, batch // lg, t, A_HEADS, A_HEAD))
    perm = tuple(range(nl)) + tuple(nl + p for p in (0, 2, 4, 1, 3))
    x = jnp.transpose(x, perm).reshape(lead + (lg, t, A_HEAD, 1, chains))
    x = jnp.broadcast_to(x, lead + (lg, t, A_HEAD, vq, chains))
    return x.reshape(lead + (lg, t, A_HEAD, vq * chains))


def _to_lanes_v(x, batch, t, lg, vq):
    gsz = A_HEAD // vq
    x = x.reshape(lg, batch // lg, t, A_HEADS, vq, gsz)
    x = jnp.transpose(x, (0, 2, 5, 4, 1, 3))
    return x.reshape(lg, t, gsz, vq * (batch // lg) * A_HEADS)


def _from_lanes_v(o, batch, t, lg, vq):
    lead = o.shape[:-4]
    gsz = A_HEAD // vq
    o = o.reshape(lead + (lg, t, gsz, vq, batch // lg, A_HEADS))
    nl = len(lead)
    perm = tuple(range(nl)) + tuple(nl + p for p in (0, 4, 1, 5, 3, 2))
    return jnp.transpose(o, perm).reshape(lead + (batch * t, D_HALF))


def _state_to_lanes(s, lg, vq):
    batch = s.shape[0]
    gsz = A_HEAD // vq
    s = s.reshape(lg, batch // lg, 2, A_HEADS, vq, gsz, A_HEAD)
    s = jnp.transpose(s, (2, 0, 6, 5, 4, 1, 3))
    return s.reshape(2, lg, A_HEAD, gsz, LANES)


def _state_from_lanes(s, batch, lg, vq):
    gsz = A_HEAD // vq
    s = s.reshape(2, lg, A_HEAD, gsz, vq, batch // lg, A_HEADS)
    s = jnp.transpose(s, (1, 5, 0, 6, 4, 3, 2))
    return s.reshape(batch, 2, A_HEADS, A_HEAD, A_HEAD)


def _even_post_kernel(x_ref, o_ref_in, bonus_ref, g_ref, bg_ref, z_ref, zprev_ref, znext_ref,
                      lng_ref, lnb_ref, cw_ref, ones_ref, gate_ref, w_ref, out_ref,
                      *, tm, np_tiles, seq_tiles, dec_tiles):
    i = pl.program_id(0)
    ones = ones_ref[...]
    inv = 1.0 / A_HEAD
    o = o_ref_in[0] + o_ref_in[1]
    mu = _dot_exact01(o, ones) * inv
    xc = o - mu
    var = _dot_exact01(xc * xc, ones) * inv
    on = xc * lax.rsqrt(var + GN_EPS) * lng_ref[...] + lnb_ref[...]
    ya = (on + bonus_ref[...]) * g_ref[...]

    pos = jnp.where(i < np_tiles, i % seq_tiles, (i - np_tiles) % dec_tiles)
    last = jnp.where(i < np_tiles, seq_tiles - 1, dec_tiles - 1)
    z = z_ref[...]
    rows = lax.broadcasted_iota(jnp.int32, z.shape, 0)
    prev_row = jnp.where(pos == 0, 0.0, zprev_ref[pl.ds(SUBLANES - 1, 1), :])
    next_row = jnp.where(pos == last, 0.0, znext_ref[pl.ds(0, 1), :])
    zp = jnp.where(rows == 0, prev_row, pltpu.roll(z, 1, axis=0))
    zn = jnp.where(rows == tm - 1, next_row, pltpu.roll(z, tm - 1, axis=0))
    yb = bg_ref[...] * (zp * cw_ref[pl.ds(0, 1), :] + z * cw_ref[pl.ds(1, 1), :] + zn * cw_ref[pl.ds(2, 1), :])

    h = D_HALF
    y = _dot(ya.astype(BF16), w_ref[pl.ds(0, h), :]) + _dot(yb.astype(BF16), w_ref[pl.ds(h, h), :])
    out_ref[...] = x_ref[...] + gate_ref[...] * y


def _even_post(x, o2, bonus, g, y, z, ln_g, ln_b, conv_w, gate, w_out, *, tm, n_prompt, seq, dec_seq):
    n, d = x.shape
    h = D_HALF
    nblk8 = n // SUBLANES
    per8 = tm // SUBLANES
    tok = pl.BlockSpec((tm, h), lambda i: (i, 0))
    full2 = lambda s: pl.BlockSpec(s, lambda i: (0, 0))
    mod_map = lambda i: (_mod_row(i, tm, n_prompt, dec_seq), 0, 0)
    return pl.pallas_call(
        functools.partial(_even_post_kernel, tm=tm, np_tiles=n_prompt // tm, seq_tiles=seq // tm,
                          dec_tiles=dec_seq // tm),
        grid=(n // tm,),
        in_specs=[
            pl.BlockSpec((tm, d), lambda i: (i, 0)),
            pl.BlockSpec((2, tm, h), lambda i: (0, i, 0)),
            tok, tok,
            pl.BlockSpec((tm, h), lambda i: (i, 3)),
            tok,
            pl.BlockSpec((SUBLANES, h), lambda i: (jnp.maximum(i * per8 - 1, 0), 0)),
            pl.BlockSpec((SUBLANES, h), lambda i: (jnp.minimum((i + 1) * per8, nblk8 - 1), 0)),
            full2((1, h)), full2((1, h)), full2((SUBLANES, h)), full2((h, h)),
            pl.BlockSpec((None, 1, d), mod_map),
            full2((d, d)),
        ],
        out_specs=pl.BlockSpec((tm, d), lambda i: (i, 0)),
        out_shape=jax.ShapeDtypeStruct((n, d), F32),
        compiler_params=_cparams("parallel"),
        name="even_post",
    )(x, o2, bonus, g, y, z, z, z, ln_g.reshape(1, h), ln_b.reshape(1, h),
      jnp.pad(conv_w, ((0, SUBLANES - conv_w.shape[0]), (0, 0))), _head_ones(), gate, w_out.astype(BF16))


def _attn_kernel(lam_ref, q_ref, k_ref, v_ref, subln_ref, o_ref, *, out_scale):
    lam = lam_ref[0]
    hd2 = 2 * C_HD
    for h in range(C_HEADS):
        ps = []
        for j in range(2):
            lo = h * hd2 + j * C_HD
            s = _dot(q_ref[:, lo:lo + C_HD], k_ref[:, lo:lo + C_HD], NT)
            e = jnp.exp(s - jnp.max(s, axis=-1, keepdims=True))
            ps.append((e, 1.0 / jnp.sum(e, axis=-1, keepdims=True)))
        attn = ps[0][0] * ps[0][1] - ps[1][0] * (lam * ps[1][1])
        oh = _dot(attn.astype(BF16), v_ref[:, h * hd2:(h + 1) * hd2])
        y = oh * lax.rsqrt(jnp.mean(oh * oh, axis=-1, keepdims=True) + NORM_EPS)
        o_ref[:, h * hd2:(h + 1) * hd2] = y * subln_ref[...] * out_scale


def _diff_attention(lam, q, k, v, subln, lam_init, *, batch, tq_total, tq):
    d = q.shape[1]
    tk = k.shape[1]
    nq = tq_total // tq
    return pl.pallas_call(
        functools.partial(_attn_kernel, out_scale=1.0 - lam_init),
        grid=(batch, nq),
        in_specs=[
            pl.BlockSpec(memory_space=pltpu.SMEM),
            pl.BlockSpec((tq, d), lambda b, i: (b * nq + i, 0)),
            pl.BlockSpec((None, tk, d), lambda b, i: (b, 0, 0)),
            pl.BlockSpec((None, tk, d), lambda b, i: (b, 0, 0)),
            pl.BlockSpec((1, 2 * C_HD), lambda b, i: (0, 0)),
        ],
        out_specs=pl.BlockSpec((tq, d), lambda b, i: (b * nq + i, 0)),
        out_shape=jax.ShapeDtypeStruct((batch * tq_total, d), F32),
        compiler_params=_cparams("parallel", "parallel"),
        name="diff_attn",
    )(lam.reshape(1), q, k, v, subln.reshape(1, 2 * C_HD))


def _axial_rope_tables(n_tok):
    rows = n_tok // GRID_W
    row = jnp.repeat(jnp.arange(rows, dtype=F32), GRID_W)
    col = jnp.tile(jnp.arange(GRID_W, dtype=F32), rows)
    nf = C_HD // 4
    inv = ROPE_BASE ** (-jnp.arange(nf, dtype=F32) / nf)
    ar = row[:, None] * inv[None, :]
    ac = col[:, None] * inv[None, :]
    ang = jnp.concatenate([ar, ar, ac, ac], axis=-1)
    return jnp.cos(ang), jnp.sin(ang)


def _apply_rope(x, cos, sin):
    h = x.shape[-1] // 2
    q = h // 2
    xr = jnp.concatenate([-x[..., q:h], x[..., :q], -x[..., h + q:], x[..., h:h + q]], axis=-1)
    return x * cos[None, :, None, None, :] + xr * sin[None, :, None, None, :]


N_TOP = P_TOPK + 1
CAND_PAIRS = [(i, j) for i in range(N_TOP) for j in range(N_TOP) if (i + 1) * (j + 1) <= N_TOP]
CAND_ROWS = -(-len(CAND_PAIRS) // SUBLANES) * SUBLANES
TOP_ROWS = -(-N_TOP // SUBLANES) * SUBLANES


def _take_top(x, n, emit):
    for it in range(n):
        m = jnp.max(x, axis=0, keepdims=True)
        emit(it, m)
        if it + 1 < n:
            x = jnp.where(x == m, NEG_INF, x)


def _peer_select_kernel(q_ref, skh_ref, skl_ref, t_out, c1_out, s2_out, e2_out, top_scr, cand_scr):
    half = P_QDIM // 2
    for h in range(P_HEADS):
        sT = []
        for j in range(2):
            hj = 2 * h + j
            qs = q_ref[:, hj * half:(hj + 1) * half]
            q_hi, q_lo = _split2(qs)
            s = _dot(skh_ref[hj], q_hi, NT) + (_dot(skh_ref[hj], q_lo, NT) + _dot(skl_ref[hj], q_hi, NT))
            sT.append(s)

            def emit(it, m, j=j):
                top_scr[j, pl.ds(it, 1), :] = m
            _take_top(s, N_TOP, emit)

        cand_scr[...] = jnp.full(cand_scr.shape, NEG_INF, F32)
        for c, (i, j) in enumerate(CAND_PAIRS):
            cand_scr[pl.ds(c, 1), :] = top_scr[0, pl.ds(i, 1), :] + top_scr[1, pl.ds(j, 1), :]
        a1 = top_scr[0, pl.ds(0, 1), :]
        b1 = top_scr[1, pl.ds(0, 1), :]
        mx = a1 + b1
        st = {"z": None, "c16": None, "c17": None}

        def emit_c(it, m):
            if it < P_TOPK:
                e = jnp.exp(m - mx)
                st["z"] = e if st["z"] is None else st["z"] + e
            if it == P_TOPK - 1:
                st["c16"] = m
            if it == P_TOPK:
                st["c17"] = m
        _take_top(cand_scr[...], N_TOP, emit_c)
        tau = 0.5 * (st["c16"] + st["c17"])
        t_out[h] = tau - sT[0]
        c1_out[h] = jnp.exp(sT[0] - a1) / st["z"]
        s2_out[h] = sT[1]
        e2_out[h] = jnp.exp(sT[1] - b1)


def _peer_select(q, sub_keys, *, tm):
    n = q.shape[0]
    sk = sub_keys.reshape(2 * P_HEADS, P_NKEYS, P_QDIM // 2)
    skh, skl = _split2(sk)
    out = pl.BlockSpec((P_HEADS, P_NKEYS, tm), lambda i: (0, 0, i))
    sds = jax.ShapeDtypeStruct((P_HEADS, P_NKEYS, n), F32)
    return pl.pallas_call(
        _peer_select_kernel,
        grid=(n // tm,),
        in_specs=[pl.BlockSpec((tm, P_HEADS * P_QDIM), lambda i: (i, 0)),
                  pl.BlockSpec(sk.shape, lambda i: (0, 0, 0)),
                  pl.BlockSpec(sk.shape, lambda i: (0, 0, 0))],
        out_specs=[out, out, out, out],
        out_shape=[sds, sds, sds, sds],
        scratch_shapes=[pltpu.VMEM((2, TOP_ROWS, tm), F32), pltpu.VMEM((CAND_ROWS, tm), F32)],
        compiler_params=_cparams("parallel"),
        name="peer_select",
    )(q, skh, skl)


PAIR = 2
PAIR_ROWS = PAIR * P_NKEYS
I2_HALF = P_NKEYS // 2


def _gelu(x):
    return 0.5 * x * (1.0 + lax.erf(x * (1.0 / math.sqrt(2.0))))


def _peer_dense_kernel(hb_ref, x_ref, gate_ref, ufirst_ref, us_ref, vts_ref, vlast_ref,
                       t_ref, c1_ref, s2_ref, e2_ref, o_ref, acc_ref, a_scr, act_scr, *, stages, tm):
    c = pl.program_id(1)

    @pl.when(c == 0)
    def _():
        acc_ref[...] = jnp.zeros(acc_ref.shape, F32)
        a_scr[stages - 1] = jnp.zeros(a_scr.shape[1:], BF16)
        act_scr[0] = _gelu(_dot(ufirst_ref[...], hb_ref[...], NT))

    def stage(p, carry):
        prev = (p + stages - 1) % stages
        nxt = (p + 1) % stages
        acc_ref[...] += _dot(vts_ref[p], a_scr[prev])
        for lg in range(tm // LANES):
            ls = slice(lg * LANES, (lg + 1) * LANES)
            for half in range(P_NKEYS // I2_HALF):
                rows = slice(half * I2_HALF, (half + 1) * I2_HALF)
                w = [None] * PAIR
                for h in range(P_HEADS):
                    s2 = s2_ref[h, rows, ls]
                    e2 = e2_ref[h, rows, ls]
                    for j in range(PAIR):
                        t = t_ref[p, pl.ds(PAIR * h + j, 1), ls]
                        c1 = c1_ref[p, pl.ds(PAIR * h + j, 1), ls]
                        term = jnp.where(s2 >= t, e2 * c1, 0.0)
                        w[j] = term if w[j] is None else w[j] + term
                for j in range(PAIR):
                    rs = slice(j * P_NKEYS + half * I2_HALF, j * P_NKEYS + (half + 1) * I2_HALF)
                    a_scr[p, rs, ls] = (w[j] * act_scr[p, rs, ls]).astype(BF16)
        r0 = pl.multiple_of(p * PAIR_ROWS, PAIR_ROWS)
        act_scr[nxt] = _gelu(_dot(us_ref[pl.ds(r0, PAIR_ROWS), :], hb_ref[...], NT))
        return carry

    lax.fori_loop(0, stages, stage, 0)

    @pl.when(c == pl.num_programs(1) - 1)
    def _():
        total = acc_ref[...] + _dot(vlast_ref[0], a_scr[stages - 1])
        o_ref[...] = x_ref[...] + gate_ref[...] * total.T


def _peer_dense(x, hb, gate, u, v, t, c1, s2, e2, *, tm, stages, n_prompt, dec_seq):
    n, d = x.shape
    ec = stages * PAIR_ROWS
    n_pairs = P_EXPERTS // PAIR_ROWS
    u_bf = u.astype(BF16)
    u_shift = jnp.concatenate([u_bf[PAIR_ROWS:], jnp.zeros((PAIR_ROWS, d), BF16)], axis=0)
    vt = jnp.transpose(v.astype(BF16).reshape(n_pairs, PAIR_ROWS, d), (0, 2, 1))
    vt_shift = jnp.concatenate([jnp.zeros((1, d, PAIR_ROWS), BF16), vt], axis=0)
    mod_map = lambda i, c: (_mod_row(i, tm, n_prompt, dec_seq), 0, 0)
    def pair_major(a):
        a = a.reshape(P_HEADS, n_pairs, PAIR, n)
        return jnp.transpose(a, (1, 0, 2, 3)).reshape(n_pairs, P_HEADS * PAIR, n)
    t, c1 = pair_major(t), pair_major(c1)
    small = pl.BlockSpec((stages, P_HEADS * PAIR, tm), lambda i, c: (c, 0, i))
    big = pl.BlockSpec((P_HEADS, P_NKEYS, tm), lambda i, c: (0, 0, i))
    return pl.pallas_call(
        functools.partial(_peer_dense_kernel, stages=stages, tm=tm),
        grid=(n // tm, P_EXPERTS // ec),
        in_specs=[
            pl.BlockSpec((tm, d), lambda i, c: (i, 0)),
            pl.BlockSpec((tm, d), lambda i, c: (i, 0)),
            pl.BlockSpec((None, 1, d), mod_map),
            pl.BlockSpec((PAIR_ROWS, d), lambda i, c: (0, 0)),
            pl.BlockSpec((ec, d), lambda i, c: (c, 0)),
            pl.BlockSpec((stages, d, PAIR_ROWS), lambda i, c: (c, 0, 0)),
            pl.BlockSpec((1, d, PAIR_ROWS), lambda i, c: (n_pairs, 0, 0)),
            small, small, big, big,
        ],
        out_specs=pl.BlockSpec((tm, d), lambda i, c: (i, 0)),
        out_shape=jax.ShapeDtypeStruct((n, d), F32),
        scratch_shapes=[pltpu.VMEM((d, tm), F32), pltpu.VMEM((stages, PAIR_ROWS, tm), BF16),
                        pltpu.VMEM((stages, PAIR_ROWS, tm), F32)],
        compiler_params=_cparams("parallel", "arbitrary"),
        name="peer_dense",
    )(hb, x, gate, u_bf, u_shift, vt_shift, vt_shift, t, c1, s2, e2)


def _final_norm_kernel(x_ref, g_ref, o_ref):
    x = x_ref[...]
    o_ref[...] = x * lax.rsqrt(jnp.mean(x * x, axis=-1, keepdims=True) + NORM_EPS) * g_ref[...]


def _final_norm(x, g, *, tm):
    n, d = x.shape
    return pl.pallas_call(
        _final_norm_kernel,
        grid=(n // tm,),
        in_specs=[pl.BlockSpec((tm, d), lambda i: (i, 0)), pl.BlockSpec((1, d), lambda i: (0, 0))],
        out_specs=pl.BlockSpec((tm, d), lambda i: (i, 0)),
        out_shape=jax.ShapeDtypeStruct((n, d), F32),
        compiler_params=_cparams("parallel"),
        name="final_norm",
    )(x, g.reshape(1, d))


def _tiles(seq, dec_seq):
    tm = math.gcd(math.gcd(seq, dec_seq), 256)
    tm_dense = math.gcd(math.gcd(seq, dec_seq), 512)
    return tm, tm_dense


def kernel(x_prompt, x_sample, c, state_rwkv, cache_k, cache_v, c_ctx, w_ada, b_ada, norm_mix, norm_ffn, w_in_even, rwkv_w0, rwkv_w_up, rwkv_a0, rwkv_a_up, rwkv_g_up, rwkv_k_k, rwkv_k_a, rwkv_r_k, rwkv_ln_g, rwkv_ln_b, conv_w, w_out_even, w_in_odd, diff_lam, diff_subln, w_out_odd, peer_w_q, peer_sub_keys, peer_u, peer_v, final_norm):
    bp, seq, d = x_prompt.shape
    bs, dec_seq, _ = x_sample.shape
    depth = w_ada.shape[0]
    n_prompt = bp * seq
    n_sample = bs * dec_seq
    tm, tm_dense = _tiles(seq, dec_seq)
    tm_dense = math.gcd(math.gcd(n_prompt, dec_seq), 512)
    tok = dict(n_prompt=n_prompt, dec_seq=dec_seq)

    x = jnp.concatenate([x_prompt.reshape(n_prompt, d), x_sample.reshape(n_sample, d)], axis=0)

    rows = 1 + bs
    rows_pad = -(-rows // SUBLANES) * SUBLANES
    cond = jnp.zeros((rows_pad, d), F32).at[0].set(c_ctx).at[1:rows].set(c)
    mod = _adaln_all(cond, w_ada, b_ada)[:, :rows].reshape(depth, rows, 6, 1, d)
    mods = [[mod[l, :, m] for m in range(6)] for l in range(depth)]

    h = D_HALF
    perm = jnp.concatenate([jnp.arange(0, 3 * h), jnp.arange(3 * h + LORA_W, 6 * h + LORA_W),
                            jnp.arange(3 * h, 3 * h + LORA_W)])

    lg_p, vq_p = _chain_layout(bp)
    lg_s, vq_s = _chain_layout(bs)
    cos, sin = _axial_rope_tables(dec_seq)
    scale = C_HD ** -0.5

    new_s, new_k, new_v = [], [], []
    for l in range(depth):
        i = l // 2
        sh1, sc1, g1, sh2, sc2, g2 = mods[l]
        if l % 2 == 0:
            y = _norm_mod_matmul(x, norm_mix[l], sc1, sh1, w_in_even[i][:, perm], tm=tm, passes=1, name="in_even", **tok)
            kk, w, kd, kka, bonus, g, z = _rwkv_prep(
                y, rwkv_w0[i], rwkv_w_up[i], rwkv_a0[i], rwkv_a_up[i], rwkv_g_up[i],
                rwkv_k_k[i], rwkv_k_a[i], rwkv_r_k[i].reshape(-1), tm=tm)
            r = y[:, 0:h]
            v = y[:, 2 * h:3 * h]
            outs = []
            for (lo, n_tok, batch, t, lg, vq, s0) in (
                    (0, n_prompt, bp, seq, lg_p, vq_p, None),
                    (n_prompt, n_sample, bs, dec_seq, lg_s, vq_s, state_rwkv[:, i])):
                sl = slice(lo, lo + n_tok)
                lk = lambda a: _to_lanes_k(a, batch, t, lg, vq)
                gsz = A_HEAD // vq
                s0l = (jnp.zeros((2, lg, A_HEAD, gsz, LANES), F32) if s0 is None
                       else _state_to_lanes(s0, lg, vq))
                o, sf = _wkv_scan(lk(r[sl]), lk(kk[sl]), _to_lanes_v(v[sl], batch, t, lg, vq),
                                  lk(w[:, sl]), lk(kd[:, sl]), lk(kka[:, sl]), s0l, tc=min(16, t))
                outs.append(_from_lanes_v(o, batch, t, lg, vq))
                if s0 is None:
                    new_s.append(_state_from_lanes(sf, batch, lg, vq))
            o2 = jnp.concatenate(outs, axis=1)
            x = _even_post(x, o2, bonus, g, y, z, rwkv_ln_g[i], rwkv_ln_b[i], conv_w[i], g1, w_out_even[i],
                           tm=tm, n_prompt=n_prompt, seq=seq, dec_seq=dec_seq)
        else:
            lam_init = 0.8 - 0.6 * math.exp(-0.3 * l)
            lp = diff_lam[i].astype(F32)
            lam = jnp.exp(jnp.sum(lp[0] * lp[1])) - jnp.exp(jnp.sum(lp[2] * lp[3])) + lam_init
            y = _norm_mod_matmul(x, norm_mix[l], sc1, sh1, w_in_odd[i], tm=tm, passes=1, name="in_odd", **tok)
            q, k, v = y[:, 0:d], y[:, d:2 * d], y[:, 2 * d:3 * d]
            new_k.append(k[:n_prompt].reshape(bp, seq, C_HEADS, 2, C_HD))
            new_v.append(v[:n_prompt].reshape(bp, seq, C_HEADS, 2 * C_HD))
            o_p = _diff_attention(lam, (q[:n_prompt] * scale).astype(BF16),
                                  k[:n_prompt].reshape(bp, seq, d).astype(BF16),
                                  v[:n_prompt].reshape(bp, seq, d).astype(BF16),
                                  diff_subln[i], lam_init, batch=bp, tq_total=seq, tq=min(seq, 256))
            shp = (bs, dec_seq, C_HEADS, 2, C_HD)
            qs = _apply_rope(q[n_prompt:].reshape(shp), cos, sin).reshape(n_sample, d)
            ks = _apply_rope(k[n_prompt:].reshape(shp), cos, sin).reshape(bs, dec_seq, d)
            k_all = jnp.concatenate([ks, cache_k[:, i].reshape(bs, -1, d)], axis=1).astype(BF16)
            v_all = jnp.concatenate([v[n_prompt:].reshape(bs, dec_seq, d), cache_v[:, i].reshape(bs, -1, d)],
                                    axis=1).astype(BF16)
            o_s = _diff_attention(lam, (qs * scale).astype(BF16), k_all, v_all, diff_subln[i], lam_init,
                                  batch=bs, tq_total=dec_seq, tq=min(dec_seq, 256))
            x = _proj_residual(x, jnp.concatenate([o_p, o_s], axis=0), g1, w_out_odd[i], tm=tm,
                               name="out_odd", **tok)

        qp, hb = _norm_mod_matmul(x, norm_ffn[l], sc2, sh2, peer_w_q[l], tm=tm, passes=3, emit_h=True,
                                  name="peer_q", **tok)
        t_, c1_, s2_, e2_ = _peer_select(qp, peer_sub_keys[l], tm=tm)
        x = _peer_dense(x, hb, g2, peer_u[l], peer_v[l], t_, c1_, s2_, e2_, tm=tm_dense, stages=4, **tok)

    yfin = _final_norm(x, final_norm, tm=tm)
    y_prompt = yfin[:n_prompt].reshape(bp, seq, d)
    y_sample = yfin[n_prompt:].reshape(bs, dec_seq, d)
    return (y_prompt, y_sample, jnp.stack(new_s, axis=1), jnp.stack(new_k, axis=1), jnp.stack(new_v, axis=1))
```

```python
import functools
import math

import jax
import jax.numpy as jnp
from jax import lax
from jax.experimental import pallas as pl
from jax.experimental.pallas import tpu as pltpu

D_MODEL = 1024
GRID_W = 64
ROPE_BASE = 10000.0
NORM_EPS = 1e-6
D_HALF = D_MODEL // 2
A_HEAD = 64
A_HEADS = D_HALF // A_HEAD
A_DECAY_LORA = 64
A_ICLR_LORA = 64
A_GATE_LORA = 128
GN_EPS = 64e-5
C_HEADS = 8
C_HD = D_MODEL // C_HEADS // 2
P_HEADS = 8
P_NKEYS = 128
P_EXPERTS = P_NKEYS * P_NKEYS
P_QDIM = 256
P_TOPK = 16
LORA_W = A_DECAY_LORA + A_ICLR_LORA + A_GATE_LORA
EVEN_IN = 6 * D_HALF + LORA_W

LANES = 128
SUBLANES = 8
VMEM_LIMIT = 56 * 1024 * 1024

F32 = jnp.float32
BF16 = jnp.bfloat16
NN = (((1,), (0,)), ((), ()))
NT = (((1,), (1,)), ((), ()))
NEG_INF = float("-inf")


def _cparams(*sem):
    return pltpu.CompilerParams(dimension_semantics=sem, vmem_limit_bytes=VMEM_LIMIT)


def _split2(a):
    hi = a.astype(BF16)
    lo = (a - hi.astype(F32)).astype(BF16)
    return hi, lo


def _split3(a):
    hi = a.astype(BF16)
    r = a - hi.astype(F32)
    mid = r.astype(BF16)
    lo = (r - mid.astype(F32)).astype(BF16)
    return hi, mid, lo


def _dot(a, b, dims=NN):
    return lax.dot_general(a, b, dims, preferred_element_type=F32)


def _dot3(a, b_hi, b_lo, dims=NN):
    a_hi, a_lo = _split2(a)
    return _dot(a_hi, b_hi, dims) + (_dot(a_hi, b_lo, dims) + _dot(a_lo, b_hi, dims))


def _wparts(w, passes):
    return (w.astype(BF16),) if passes == 1 else _split2(w)


def _wdot(a, parts, dims=NN):
    if len(parts) == 1:
        return _dot(a.astype(BF16), parts[0], dims)
    return _dot3(a, parts[0], parts[1], dims)


def _dot_exact01(a, ones_bf16):
    hi, mid, lo = _split3(a)
    return _dot(hi, ones_bf16) + (_dot(mid, ones_bf16) + _dot(lo, ones_bf16))


def _mod_row(i, tm, n_prompt, dec_seq):
    np_tiles = n_prompt // tm
    per = dec_seq // tm
    return jnp.where(i < np_tiles, 0, 1 + (i - np_tiles) // per)


def _ada_kernel(c_ref, whi_ref, wlo_ref, b_ref, o_ref):
    c = c_ref[...]
    s = c * jax.nn.sigmoid(c)
    o_ref[0] = _dot3(s, whi_ref[0], wlo_ref[0]) + b_ref[0]


def _adaln_all(cond, w_ada, b_ada):
    depth, d, n6 = w_ada.shape
    rows = cond.shape[0]
    tn = n6 // 4
    whi, wlo = _split2(w_ada)
    return pl.pallas_call(
        _ada_kernel,
        grid=(depth, n6 // tn),
        in_specs=[
            pl.BlockSpec((rows, d), lambda l, j: (0, 0)),
            pl.BlockSpec((1, d, tn), lambda l, j: (l, 0, j)),
            pl.BlockSpec((1, d, tn), lambda l, j: (l, 0, j)),
            pl.BlockSpec((1, 1, tn), lambda l, j: (l, 0, j)),
        ],
        out_specs=pl.BlockSpec((1, rows, tn), lambda l, j: (l, 0, j)),
        out_shape=jax.ShapeDtypeStruct((depth, rows, n6), F32),
        compiler_params=_cparams("parallel", "parallel"),
        name="adaln",
    )(cond, whi, wlo, b_ada.reshape(depth, 1, n6))


def _nmm_kernel(x_ref, g_ref, sc_ref, sh_ref, *rest, n_w):
    w_refs, o_ref, maybe_h = rest[:n_w], rest[n_w], rest[n_w + 1:]
    x = x_ref[...]
    y = x * lax.rsqrt(jnp.mean(x * x, axis=-1, keepdims=True) + NORM_EPS)
    h = y * g_ref[...] * (1.0 + sc_ref[...]) + sh_ref[...]
    o_ref[...] = _wdot(h, tuple(r[...] for r in w_refs))
    if maybe_h:
        maybe_h[0][...] = h.T.astype(BF16)


def _norm_mod_matmul(x, gnorm, scale, shift, w, *, tm, n_prompt, dec_seq, passes, emit_h=False, name):
    n, d = x.shape
    nout = w.shape[1]
    tn = nout if emit_h else nout // 2
    wp = _wparts(w, passes)
    mod_map = lambda j, i: (_mod_row(i, tm, n_prompt, dec_seq), 0, 0)
    out_shape = [jax.ShapeDtypeStruct((n, nout), F32)]
    out_specs = [pl.BlockSpec((tm, tn), lambda j, i: (i, j))]
    if emit_h:
        out_shape.append(jax.ShapeDtypeStruct((d, n), BF16))
        out_specs.append(pl.BlockSpec((d, tm), lambda j, i: (0, i)))
    res = pl.pallas_call(
        functools.partial(_nmm_kernel, n_w=len(wp)),
        grid=(nout // tn, n // tm),
        in_specs=[
            pl.BlockSpec((tm, d), lambda j, i: (i, 0)),
            pl.BlockSpec((1, d), lambda j, i: (0, 0)),
            pl.BlockSpec((None, 1, d), mod_map),
            pl.BlockSpec((None, 1, d), mod_map),
        ] + [pl.BlockSpec((d, tn), lambda j, i: (0, j))] * len(wp),
        out_specs=out_specs,
        out_shape=out_shape,
        compiler_params=_cparams("arbitrary", "arbitrary"),
        name=name,
    )(x, gnorm.reshape(1, d), scale, shift, *wp)
    return res if emit_h else res[0]


def _proj_res_kernel(x_ref, a_ref, gate_ref, w_ref, o_ref):
    o_ref[...] = x_ref[...] + gate_ref[...] * _dot(a_ref[...].astype(BF16), w_ref[...])


def _proj_residual(x, a, gate, w, *, tm, n_prompt, dec_seq, name):
    n, d = x.shape
    k = a.shape[1]
    mod_map = lambda i: (_mod_row(i, tm, n_prompt, dec_seq), 0, 0)
    return pl.pallas_call(
        _proj_res_kernel,
        grid=(n // tm,),
        in_specs=[
            pl.BlockSpec((tm, d), lambda i: (i, 0)),
            pl.BlockSpec((tm, k), lambda i: (i, 0)),
            pl.BlockSpec((None, 1, d), mod_map),
            pl.BlockSpec((k, d), lambda i: (0, 0)),
        ],
        out_specs=pl.BlockSpec((tm, d), lambda i: (i, 0)),
        out_shape=jax.ShapeDtypeStruct((n, d), F32),
        compiler_params=_cparams("parallel"),
        name=name,
    )(x, a, gate, w.astype(BF16))


def _rwkv_prep_kernel(r_ref, k_ref, v_ref, cg_ref, hc_ref, lora_ref,
                      w0_ref, wuph_ref, wupl_ref, a0_ref, auph_ref, aupl_ref, guph_ref, gupl_ref,
                      kk_ref_p, ka_ref_p, rk_ref_p, ones_ref,
                      kk_out, w_out, kd_out, kka_out, bonus_out, g_out, z_out):
    r = r_ref[...]
    k = k_ref[...]
    v = v_ref[...]
    lora = lora_ref[...]
    ones = ones_ref[...]
    wd = lora[:, :A_DECAY_LORA]
    ad = lora[:, A_DECAY_LORA:A_DECAY_LORA + A_ICLR_LORA]
    gd = lora[:, A_DECAY_LORA + A_ICLR_LORA:]

    kk = k * kk_ref_p[...]
    nrm = jnp.sqrt(_dot_exact01(kk * kk, ones))
    kk = kk / jnp.maximum(nrm, 1e-12)
    kk_out[...] = kk

    tw = jnp.tanh(wd)
    bonus = None
    for d in range(2):
        w_pre = w0_ref[d] + _dot3(tw, wuph_ref[d], wupl_ref[d])
        w_out[d] = jnp.exp(-(math.exp(-0.5) * jax.nn.sigmoid(w_pre)))
        a = jax.nn.sigmoid(a0_ref[d] + _dot3(ad, auph_ref[d], aupl_ref[d]))
        kd = k * (1.0 + (a - 1.0) * ka_ref_p[...])
        kd_out[d] = kd
        kka_out[d] = kk * a
        b = _dot_exact01(r * kd * rk_ref_p[...], ones) * v
        bonus = b if bonus is None else bonus + b
    bonus_out[...] = bonus
    g_out[...] = _dot3(jax.nn.sigmoid(gd), guph_ref[...], gupl_ref[...])
    z_out[...] = cg_ref[...] * hc_ref[...]


def _head_ones():
    idx = jnp.arange(D_HALF) // A_HEAD
    return (idx[:, None] == idx[None, :]).astype(BF16)


def _rwkv_prep(y, w0, w_up, a0, a_up, g_up, k_k, k_a, r_k, *, tm):
    n = y.shape[0]
    h = D_HALF
    wuph, wupl = _split2(w_up)
    auph, aupl = _split2(a_up)
    guph, gupl = _split2(g_up)
    col = lambda c: pl.BlockSpec((tm, h), lambda i: (i, c))
    full2 = lambda s: pl.BlockSpec(s, lambda i: (0, 0))
    full3 = lambda s: pl.BlockSpec(s, lambda i: (0, 0, 0))
    tok = pl.BlockSpec((tm, h), lambda i: (i, 0))
    tok2 = pl.BlockSpec((2, tm, h), lambda i: (0, i, 0))
    sds = lambda *s: jax.ShapeDtypeStruct(s, F32)
    return pl.pallas_call(
        _rwkv_prep_kernel,
        grid=(n // tm,),
        in_specs=[col(0), col(1), col(2), col(4), col(5),
                  pl.BlockSpec((tm, LORA_W), lambda i: (i, 6 * h // LORA_W)),
                  full3((2, 1, h)), full3((2, A_DECAY_LORA, h)), full3((2, A_DECAY_LORA, h)),
                  full3((2, 1, h)), full3((2, A_ICLR_LORA, h)), full3((2, A_ICLR_LORA, h)),
                  full2((A_GATE_LORA, h)), full2((A_GATE_LORA, h)),
                  full2((1, h)), full2((1, h)), full2((1, h)), full2((h, h))],
        out_specs=[tok, tok2, tok2, tok2, tok, tok, tok],
        out_shape=[sds(n, h), sds(2, n, h), sds(2, n, h), sds(2, n, h), sds(n, h), sds(n, h), sds(n, h)],
        compiler_params=_cparams("parallel"),
        name="rwkv_prep",
    )(y, y, y, y, y, y,
      w0.reshape(2, 1, h), wuph, wupl, a0.reshape(2, 1, h), auph, aupl, guph, gupl,
      k_k.reshape(1, h), k_a.reshape(1, h), r_k.reshape(1, h), _head_ones())


def _scan_kernel(r_ref, kk_ref, v_ref, w_ref, kd_ref, kka_ref, s0_ref, o_ref, sf_ref, s_scr, *, tc, g):
    d = pl.program_id(0)
    c = pl.program_id(2)
    nk = A_HEAD
    parts = 4

    @pl.when(c == 0)
    def _():
        s_scr[...] = s0_ref[...]

    def row(ref, t, k):
        return jnp.broadcast_to(ref[t, pl.ds(k, 1), :], (g, LANES))

    def tree(xs):
        while len(xs) > 1:
            xs = [xs[i] + xs[i + 1] for i in range(0, len(xs) - 1, 2)] + ([xs[-1]] if len(xs) % 2 else [])
        return xs[0]

    def step(i, carry):
        t = i + d * (tc - 1 - 2 * i)
        acc = [None] * parts
        for k in range(nk):
            p = s_scr[k] * row(kk_ref, t, k)
            acc[k % parts] = p if acc[k % parts] is None else acc[k % parts] + p
        sa = tree(acc)
        vv = v_ref[t]
        acc = [None] * parts
        for k in range(nk):
            sn = s_scr[k] * row(w_ref, t, k) - sa * row(kka_ref, t, k) + vv * row(kd_ref, t, k)
            s_scr[k] = sn
            p = sn * row(r_ref, t, k)
            acc[k % parts] = p if acc[k % parts] is None else acc[k % parts] + p
        o_ref[t] = tree(acc)
        return carry

    lax.fori_loop(0, tc, step, 0)

    @pl.when(c == pl.num_programs(2) - 1)
    def _():
        sf_ref[...] = s_scr[...]


def _wkv_scan(r, kk, v, w, kd, kka, s0, *, tc):
    lg, t, nk, _ = r.shape
    g = v.shape[2]
    nt = t // tc
    tmap = lambda d, l, c: c + d * (nt - 1 - 2 * c)
    shared = lambda rows: pl.BlockSpec((None, tc, rows, LANES), lambda d, l, c: (l, tmap(d, l, c), 0, 0))
    perdir = pl.BlockSpec((None, None, tc, nk, LANES), lambda d, l, c: (d, l, tmap(d, l, c), 0, 0))
    state = pl.BlockSpec((None, None, nk, g, LANES), lambda d, l, c: (d, l, 0, 0, 0))
    return pl.pallas_call(
        functools.partial(_scan_kernel, tc=tc, g=g),
        grid=(2, lg, nt),
        in_specs=[shared(nk), shared(nk), shared(g), perdir, perdir, perdir, state],
        out_specs=[pl.BlockSpec((None, None, tc, g, LANES), lambda d, l, c: (d, l, tmap(d, l, c), 0, 0)), state],
        out_shape=[jax.ShapeDtypeStruct((2, lg, t, g, LANES), F32),
                   jax.ShapeDtypeStruct((2, lg, nk, g, LANES), F32)],
        scratch_shapes=[pltpu.VMEM((nk, g, LANES), F32)],
        compiler_params=_cparams("parallel", "parallel", "arbitrary"),
        name="wkv_scan",
    )(r, kk, v, w, kd, kka, s0)


def _chain_layout(batch):
    chains = batch * A_HEADS
    if chains >= LANES:
        assert chains % LANES == 0
        return chains // LANES, 1
    assert LANES % chains == 0 and A_HEAD % (LANES // chains) == 0
    return 1, LANES // chains


def _to_lanes_k(x, batch, t, lg, vq):
    lead = x.shape[:-2]
    nl = len(lead)
    chains = (batch // lg) * A_HEADS
    x = x.reshape(lead + (lg, batch // lg, t, A_HEADS, A_HEAD))
    perm = tuple(range(nl)) + tuple(nl + p for p in (0, 2, 4, 1, 3))
    x = jnp.transpose(x, perm).reshape(lead + (lg, t, A_HEAD, 1, chains))
    x = jnp.broadcast_to(x, lead + (lg, t, A_HEAD, vq, chains))
    return x.reshape(lead + (lg, t, A_HEAD, vq * chains))


def _to_lanes_v(x, batch, t, lg, vq):
    gsz = A_HEAD // vq
    x = x.reshape(lg, batch // lg, t, A_HEADS, vq, gsz)
    x = jnp.transpose(x, (0, 2, 5, 4, 1, 3))
    return x.reshape(lg, t, gsz, vq * (batch // lg) * A_HEADS)


def _from_lanes_v(o, batch, t, lg, vq):
    lead = o.shape[:-4]
    gsz = A_HEAD // vq
    o = o.reshape(lead + (lg, t, gsz, vq, batch // lg, A_HEADS))
    nl = len(lead)
    perm = tuple(range(nl)) + tuple(nl + p for p in (0, 4, 1, 5, 3, 2))
    return jnp.transpose(o, perm).reshape(lead + (batch * t, D_HALF))


def _state_to_lanes(s, lg, vq):
    batch = s.shape[0]
    gsz = A_HEAD // vq
    s = s.reshape(lg, batch // lg, 2, A_HEADS, vq, gsz, A_HEAD)
    s = jnp.transpose(s, (2, 0, 6, 5, 4, 1, 3))
    return s.reshape(2, lg, A_HEAD, gsz, LANES)


def _state_from_lanes(s, batch, lg, vq):
    gsz = A_HEAD // vq
    s = s.reshape(2, lg, A_HEAD, gsz, vq, batch // lg, A_HEADS)
    s = jnp.transpose(s, (1, 5, 0, 6, 4, 3, 2))
    return s.reshape(batch, 2, A_HEADS, A_HEAD, A_HEAD)


def _even_post_kernel(x_ref, o_ref_in, bonus_ref, g_ref, bg_ref, z_ref, zprev_ref, znext_ref,
                      lng_ref, lnb_ref, cw_ref, ones_ref, gate_ref, w_ref, out_ref,
                      *, tm, np_tiles, seq_tiles, dec_tiles):
    i = pl.program_id(0)
    ones = ones_ref[...]
    inv = 1.0 / A_HEAD
    o = o_ref_in[0] + o_ref_in[1]
    mu = _dot_exact01(o, ones) * inv
    xc = o - mu
    var = _dot_exact01(xc * xc, ones) * inv
    on = xc * lax.rsqrt(var + GN_EPS) * lng_ref[...] + lnb_ref[...]
    ya = (on + bonus_ref[...]) * g_ref[...]

    pos = jnp.where(i < np_tiles, i % seq_tiles, (i - np_tiles) % dec_tiles)
    last = jnp.where(i < np_tiles, seq_tiles - 1, dec_tiles - 1)
    z = z_ref[...]
    rows = lax.broadcasted_iota(jnp.int32, z.shape, 0)
    prev_row = jnp.where(pos == 0, 0.0, zprev_ref[pl.ds(SUBLANES - 1, 1), :])
    next_row = jnp.where(pos == last, 0.0, znext_ref[pl.ds(0, 1), :])
    zp = jnp.where(rows == 0, prev_row, pltpu.roll(z, 1, axis=0))
    zn = jnp.where(rows == tm - 1, next_row, pltpu.roll(z, tm - 1, axis=0))
    yb = bg_ref[...] * (zp * cw_ref[pl.ds(0, 1), :] + z * cw_ref[pl.ds(1, 1), :] + zn * cw_ref[pl.ds(2, 1), :])

    h = D_HALF
    y = _dot(ya.astype(BF16), w_ref[pl.ds(0, h), :]) + _dot(yb.astype(BF16), w_ref[pl.ds(h, h), :])
    out_ref[...] = x_ref[...] + gate_ref[...] * y


def _even_post(x, o2, bonus, g, y, z, ln_g, ln_b, conv_w, gate, w_out, *, tm, n_prompt, seq, dec_seq):
    n, d = x.shape
    h = D_HALF
    nblk8 = n // SUBLANES
    per8 = tm // SUBLANES
    tok = pl.BlockSpec((tm, h), lambda i: (i, 0))
    full2 = lambda s: pl.BlockSpec(s, lambda i: (0, 0))
    mod_map = lambda i: (_mod_row(i, tm, n_prompt, dec_seq), 0, 0)
    return pl.pallas_call(
        functools.partial(_even_post_kernel, tm=tm, np_tiles=n_prompt // tm, seq_tiles=seq // tm,
                          dec_tiles=dec_seq // tm),
        grid=(n // tm,),
        in_specs=[
            pl.BlockSpec((tm, d), lambda i: (i, 0)),
            pl.BlockSpec((2, tm, h), lambda i: (0, i, 0)),
            tok, tok,
            pl.BlockSpec((tm, h), lambda i: (i, 3)),
            tok,
            pl.BlockSpec((SUBLANES, h), lambda i: (jnp.maximum(i * per8 - 1, 0), 0)),
            pl.BlockSpec((SUBLANES, h), lambda i: (jnp.minimum((i + 1) * per8, nblk8 - 1), 0)),
            full2((1, h)), full2((1, h)), full2((SUBLANES, h)), full2((h, h)),
            pl.BlockSpec((None, 1, d), mod_map),
            full2((d, d)),
        ],
        out_specs=pl.BlockSpec((tm, d), lambda i: (i, 0)),
        out_shape=jax.ShapeDtypeStruct((n, d), F32),
        compiler_params=_cparams("parallel"),
        name="even_post",
    )(x, o2, bonus, g, y, z, z, z, ln_g.reshape(1, h), ln_b.reshape(1, h),
      jnp.pad(conv_w, ((0, SUBLANES - conv_w.shape[0]), (0, 0))), _head_ones(), gate, w_out.astype(BF16))


def _attn_kernel(lam_ref, q_ref, k_ref, v_ref, subln_ref, o_ref, *, out_scale):
    lam = lam_ref[0]
    hd2 = 2 * C_HD
    for h in range(C_HEADS):
        ps = []
        for j in range(2):
            lo = h * hd2 + j * C_HD
            s = _dot(q_ref[:, lo:lo + C_HD], k_ref[:, lo:lo + C_HD], NT)
            e = jnp.exp(s - jnp.max(s, axis=-1, keepdims=True))
            ps.append((e, 1.0 / jnp.sum(e, axis=-1, keepdims=True)))
        attn = ps[0][0] * ps[0][1] - ps[1][0] * (lam * ps[1][1])
        oh = _dot(attn.astype(BF16), v_ref[:, h * hd2:(h + 1) * hd2])
        y = oh * lax.rsqrt(jnp.mean(oh * oh, axis=-1, keepdims=True) + NORM_EPS)
        o_ref[:, h * hd2:(h + 1) * hd2] = y * subln_ref[...] * out_scale


def _diff_attention(lam, q, k, v, subln, lam_init, *, batch, tq_total, tq):
    d = q.shape[1]
    tk = k.shape[1]
    nq = tq_total // tq
    return pl.pallas_call(
        functools.partial(_attn_kernel, out_scale=1.0 - lam_init),
        grid=(batch, nq),
        in_specs=[
            pl.BlockSpec(memory_space=pltpu.SMEM),
            pl.BlockSpec((tq, d), lambda b, i: (b * nq + i, 0)),
            pl.BlockSpec((None, tk, d), lambda b, i: (b, 0, 0)),
            pl.BlockSpec((None, tk, d), lambda b, i: (b, 0, 0)),
            pl.BlockSpec((1, 2 * C_HD), lambda b, i: (0, 0)),
        ],
        out_specs=pl.BlockSpec((tq, d), lambda b, i: (b * nq + i, 0)),
        out_shape=jax.ShapeDtypeStruct((batch * tq_total, d), F32),
        compiler_params=_cparams("parallel", "parallel"),
        name="diff_attn",
    )(lam.reshape(1), q, k, v, subln.reshape(1, 2 * C_HD))


def _axial_rope_tables(n_tok):
    rows = n_tok // GRID_W
    row = jnp.repeat(jnp.arange(rows, dtype=F32), GRID_W)
    col = jnp.tile(jnp.arange(GRID_W, dtype=F32), rows)
    nf = C_HD // 4
    inv = ROPE_BASE ** (-jnp.arange(nf, dtype=F32) / nf)
    ar = row[:, None] * inv[None, :]
    ac = col[:, None] * inv[None, :]
    ang = jnp.concatenate([ar, ar, ac, ac], axis=-1)
    return jnp.cos(ang), jnp.sin(ang)


def _apply_rope(x, cos, sin):
    h = x.shape[-1] // 2
    q = h // 2
    xr = jnp.concatenate([-x[..., q:h], x[..., :q], -x[..., h + q:], x[..., h:h + q]], axis=-1)
    return x * cos[None, :, None, None, :] + xr * sin[None, :, None, None, :]


N_TOP = P_TOPK + 1
CAND_PAIRS = [(i, j) for i in range(N_TOP) for j in range(N_TOP) if (i + 1) * (j + 1) <= N_TOP]
CAND_ROWS = -(-len(CAND_PAIRS) // SUBLANES) * SUBLANES
TOP_ROWS = -(-N_TOP // SUBLANES) * SUBLANES


def _take_top(x, n, emit):
    for it in range(n):
        m = jnp.max(x, axis=0, keepdims=True)
        emit(it, m)
        if it + 1 < n:
            x = jnp.where(x == m, NEG_INF, x)


def _peer_select_kernel(q_ref, skh_ref, skl_ref, t_out, c1_out, s2_out, e2_out, top_scr, cand_scr):
    half = P_QDIM // 2
    for h in range(P_HEADS):
        sT = []
        for j in range(2):
            hj = 2 * h + j
            qs = q_ref[:, hj * half:(hj + 1) * half]
            q_hi, q_lo = _split2(qs)
            s = _dot(skh_ref[hj], q_hi, NT) + (_dot(skh_ref[hj], q_lo, NT) + _dot(skl_ref[hj], q_hi, NT))
            sT.append(s)

            def emit(it, m, j=j):
                top_scr[j, pl.ds(it, 1), :] = m
            _take_top(s, N_TOP, emit)

        cand_scr[...] = jnp.full(cand_scr.shape, NEG_INF, F32)
        for c, (i, j) in enumerate(CAND_PAIRS):
            cand_scr[pl.ds(c, 1), :] = top_scr[0, pl.ds(i, 1), :] + top_scr[1, pl.ds(j, 1), :]
        a1 = top_scr[0, pl.ds(0, 1), :]
        b1 = top_scr[1, pl.ds(0, 1), :]
        mx = a1 + b1
        st = {"z": None, "c16": None, "c17": None}

        def emit_c(it, m):
            if it < P_TOPK:
                e = jnp.exp(m - mx)
                st["z"] = e if st["z"] is None else st["z"] + e
            if it == P_TOPK - 1:
                st["c16"] = m
            if it == P_TOPK:
                st["c17"] = m
        _take_top(cand_scr[...], N_TOP, emit_c)
        tau = 0.5 * (st["c16"] + st["c17"])
        t_out[h] = tau - sT[0]
        c1_out[h] = jnp.exp(sT[0] - a1) / st["z"]
        s2_out[h] = sT[1]
        e2_out[h] = jnp.exp(sT[1] - b1)


def _peer_select(q, sub_keys, *, tm):
    n = q.shape[0]
    sk = sub_keys.reshape(2 * P_HEADS, P_NKEYS, P_QDIM // 2)
    skh, skl = _split2(sk)
    out = pl.BlockSpec((P_HEADS, P_NKEYS, tm), lambda i: (0, 0, i))
    sds = jax.ShapeDtypeStruct((P_HEADS, P_NKEYS, n), F32)
    return pl.pallas_call(
        _peer_select_kernel,
        grid=(n // tm,),
        in_specs=[pl.BlockSpec((tm, P_HEADS * P_QDIM), lambda i: (i, 0)),
                  pl.BlockSpec(sk.shape, lambda i: (0, 0, 0)),
                  pl.BlockSpec(sk.shape, lambda i: (0, 0, 0))],
        out_specs=[out, out, out, out],
        out_shape=[sds, sds, sds, sds],
        scratch_shapes=[pltpu.VMEM((2, TOP_ROWS, tm), F32), pltpu.VMEM((CAND_ROWS, tm), F32)],
        compiler_params=_cparams("parallel"),
        name="peer_select",
    )(q, skh, skl)


PAIR = 2
PAIR_ROWS = PAIR * P_NKEYS
I2_HALF = P_NKEYS // 2


def _gelu(x):
    return 0.5 * x * (1.0 + lax.erf(x * (1.0 / math.sqrt(2.0))))


def _peer_dense_kernel(hbt_ref, x_ref, gate_ref, ufirst_ref, us_ref, vts_ref, vlast_ref,
                       t_ref, c1_ref, s2_ref, e2_ref, o_ref, acc_ref, a_scr, act_scr, *, stages, tm):
    c = pl.program_id(1)

    @pl.when(c == 0)
    def _():
        acc_ref[...] = jnp.zeros(acc_ref.shape, F32)
        a_scr[stages - 1] = jnp.zeros(a_scr.shape[1:], BF16)
        act_scr[0] = _gelu(_dot(ufirst_ref[...], hbt_ref[...]))

    def stage(p, carry):
        prev = (p + stages - 1) % stages
        nxt = (p + 1) % stages
        acc_ref[...] += _dot(vts_ref[p], a_scr[prev])
        for lg in range(tm // LANES):
            ls = slice(lg * LANES, (lg + 1) * LANES)
            for half in range(P_NKEYS // I2_HALF):
                rows = slice(half * I2_HALF, (half + 1) * I2_HALF)
                w = [None] * PAIR
                for h in range(P_HEADS):
                    s2 = s2_ref[h, rows, ls]
                    e2 = e2_ref[h, rows, ls]
                    for j in range(PAIR):
                        t = t_ref[p, pl.ds(PAIR * h + j, 1), ls]
                        c1 = c1_ref[p, pl.ds(PAIR * h + j, 1), ls]
                        term = jnp.where(s2 >= t, e2 * c1, 0.0)
                        w[j] = term if w[j] is None else w[j] + term
                for j in range(PAIR):
                    rs = slice(j * P_NKEYS + half * I2_HALF, j * P_NKEYS + (half + 1) * I2_HALF)
                    a_scr[p, rs, ls] = (w[j] * act_scr[p, rs, ls]).astype(BF16)
        r0 = pl.multiple_of(p * PAIR_ROWS, PAIR_ROWS)
        act_scr[nxt] = _gelu(_dot(us_ref[pl.ds(r0, PAIR_ROWS), :], hbt_ref[...]))
        return carry

    lax.fori_loop(0, stages, stage, 0)

    @pl.when(c == pl.num_programs(1) - 1)
    def _():
        total = acc_ref[...] + _dot(vlast_ref[0], a_scr[stages - 1])
        o_ref[...] = x_ref[...] + gate_ref[...] * total.T


def _peer_dense(x, hb, gate, u, v, t, c1, s2, e2, *, tm, stages, n_prompt, dec_seq):
    n, d = x.shape
    ec = stages * PAIR_ROWS
    n_pairs = P_EXPERTS // PAIR_ROWS
    u_bf = u.astype(BF16)
    u_shift = jnp.concatenate([u_bf[PAIR_ROWS:], jnp.zeros((PAIR_ROWS, d), BF16)], axis=0)
    vt = jnp.transpose(v.astype(BF16).reshape(n_pairs, PAIR_ROWS, d), (0, 2, 1))
    vt_shift = jnp.concatenate([jnp.zeros((1, d, PAIR_ROWS), BF16), vt], axis=0)
    mod_map = lambda i, c: (_mod_row(i, tm, n_prompt, dec_seq), 0, 0)
    def pair_major(a):
        a = a.reshape(P_HEADS, n_pairs, PAIR, n)
        return jnp.transpose(a, (1, 0, 2, 3)).reshape(n_pairs, P_HEADS * PAIR, n)
    t, c1 = pair_major(t), pair_major(c1)
    small = pl.BlockSpec((stages, P_HEADS * PAIR, tm), lambda i, c: (c, 0, i))
    big = pl.BlockSpec((P_HEADS, P_NKEYS, tm), lambda i, c: (0, 0, i))
    return pl.pallas_call(
        functools.partial(_peer_dense_kernel, stages=stages, tm=tm),
        grid=(n // tm, P_EXPERTS // ec),
        in_specs=[
            pl.BlockSpec((d, tm), lambda i, c: (0, i)),
            pl.BlockSpec((tm, d), lambda i, c: (i, 0)),
            pl.BlockSpec((None, 1, d), mod_map),
            pl.BlockSpec((PAIR_ROWS, d), lambda i, c: (0, 0)),
            pl.BlockSpec((ec, d), lambda i, c: (c, 0)),
            pl.BlockSpec((stages, d, PAIR_ROWS), lambda i, c: (c, 0, 0)),
            pl.BlockSpec((1, d, PAIR_ROWS), lambda i, c: (n_pairs, 0, 0)),
            small, small, big, big,
        ],
        out_specs=pl.BlockSpec((tm, d), lambda i, c: (i, 0)),
        out_shape=jax.ShapeDtypeStruct((n, d), F32),
        scratch_shapes=[pltpu.VMEM((d, tm), F32), pltpu.VMEM((stages, PAIR_ROWS, tm), BF16),
                        pltpu.VMEM((stages, PAIR_ROWS, tm), F32)],
        compiler_params=_cparams("parallel", "arbitrary"),
        name="peer_dense",
    )(hb, x, gate, u_bf, u_shift, vt_shift, vt_shift, t, c1, s2, e2)


def _final_norm_kernel(x_ref, g_ref, o_ref):
    x = x_ref[...]
    o_ref[...] = x * lax.rsqrt(jnp.mean(x * x, axis=-1, keepdims=True) + NORM_EPS) * g_ref[...]


def _final_norm(x, g, *, tm):
    n, d = x.shape
    return pl.pallas_call(
        _final_norm_kernel,
        grid=(n // tm,),
        in_specs=[pl.BlockSpec((tm, d), lambda i: (i, 0)), pl.BlockSpec((1, d), lambda i: (0, 0))],
        out_specs=pl.BlockSpec((tm, d), lambda i: (i, 0)),
        out_shape=jax.ShapeDtypeStruct((n, d), F32),
        compiler_params=_cparams("parallel"),
        name="final_norm",
    )(x, g.reshape(1, d))


def _tiles(seq, dec_seq):
    tm = math.gcd(math.gcd(seq, dec_seq), 256)
    tm_dense = math.gcd(math.gcd(seq, dec_seq), 512)
    return tm, tm_dense


def kernel(x_prompt, x_sample, c, state_rwkv, cache_k, cache_v, c_ctx, w_ada, b_ada, norm_mix, norm_ffn, w_in_even, rwkv_w0, rwkv_w_up, rwkv_a0, rwkv_a_up, rwkv_g_up, rwkv_k_k, rwkv_k_a, rwkv_r_k, rwkv_ln_g, rwkv_ln_b, conv_w, w_out_even, w_in_odd, diff_lam, diff_subln, w_out_odd, peer_w_q, peer_sub_keys, peer_u, peer_v, final_norm):
    bp, seq, d = x_prompt.shape
    bs, dec_seq, _ = x_sample.shape
    depth = w_ada.shape[0]
    n_prompt = bp * seq
    n_sample = bs * dec_seq
    tm, tm_dense = _tiles(seq, dec_seq)
    tm_dense = math.gcd(math.gcd(n_prompt, dec_seq), 512)
    tok = dict(n_prompt=n_prompt, dec_seq=dec_seq)

    x = jnp.concatenate([x_prompt.reshape(n_prompt, d), x_sample.reshape(n_sample, d)], axis=0)

    rows = 1 + bs
    rows_pad = -(-rows // SUBLANES) * SUBLANES
    cond = jnp.zeros((rows_pad, d), F32).at[0].set(c_ctx).at[1:rows].set(c)
    mod = _adaln_all(cond, w_ada, b_ada)[:, :rows].reshape(depth, rows, 6, 1, d)
    mods = [[mod[l, :, m] for m in range(6)] for l in range(depth)]

    h = D_HALF
    perm = jnp.concatenate([jnp.arange(0, 3 * h), jnp.arange(3 * h + LORA_W, 6 * h + LORA_W),
                            jnp.arange(3 * h, 3 * h + LORA_W)])

    lg_p, vq_p = _chain_layout(bp)
    lg_s, vq_s = _chain_layout(bs)
    cos, sin = _axial_rope_tables(dec_seq)
    scale = C_HD ** -0.5

    new_s, new_k, new_v = [], [], []
    for l in range(depth):
        i = l // 2
        sh1, sc1, g1, sh2, sc2, g2 = mods[l]
        if l % 2 == 0:
            y = _norm_mod_matmul(x, norm_mix[l], sc1, sh1, w_in_even[i][:, perm], tm=tm, passes=1, name="in_even", **tok)
            kk, w, kd, kka, bonus, g, z = _rwkv_prep(
                y, rwkv_w0[i], rwkv_w_up[i], rwkv_a0[i], rwkv_a_up[i], rwkv_g_up[i],
                rwkv_k_k[i], rwkv_k_a[i], rwkv_r_k[i].reshape(-1), tm=tm)
            r = y[:, 0:h]
            v = y[:, 2 * h:3 * h]
            outs = []
            for (lo, n_tok, batch, t, lg, vq, s0) in (
                    (0, n_prompt, bp, seq, lg_p, vq_p, None),
                    (n_prompt, n_sample, bs, dec_seq, lg_s, vq_s, state_rwkv[:, i])):
                sl = slice(lo, lo + n_tok)
                lk = lambda a: _to_lanes_k(a, batch, t, lg, vq)
                gsz = A_HEAD // vq
                s0l = (jnp.zeros((2, lg, A_HEAD, gsz, LANES), F32) if s0 is None
                       else _state_to_lanes(s0, lg, vq))
                o, sf = _wkv_scan(lk(r[sl]), lk(kk[sl]), _to_lanes_v(v[sl], batch, t, lg, vq),
                                  lk(w[:, sl]), lk(kd[:, sl]), lk(kka[:, sl]), s0l, tc=min(16, t))
                outs.append(_from_lanes_v(o, batch, t, lg, vq))
                if s0 is None:
                    new_s.append(_state_from_lanes(sf, batch, lg, vq))
            o2 = jnp.concatenate(outs, axis=1)
            x = _even_post(x, o2, bonus, g, y, z, rwkv_ln_g[i], rwkv_ln_b[i], conv_w[i], g1, w_out_even[i],
                           tm=tm, n_prompt=n_prompt, seq=seq, dec_seq=dec_seq)
        else:
            lam_init = 0.8 - 0.6 * math.exp(-0.3 * l)
            lp = diff_lam[i].astype(F32)
            lam = jnp.exp(jnp.sum(lp[0] * lp[1])) - jnp.exp(jnp.sum(lp[2] * lp[3])) + lam_init
            y = _norm_mod_matmul(x, norm_mix[l], sc1, sh1, w_in_odd[i], tm=tm, passes=1, name="in_odd", **tok)
            q, k, v = y[:, 0:d], y[:, d:2 * d], y[:, 2 * d:3 * d]
            new_k.append(k[:n_prompt].reshape(bp, seq, C_HEADS, 2, C_HD))
            new_v.append(v[:n_prompt].reshape(bp, seq, C_HEADS, 2 * C_HD))
            o_p = _diff_attention(lam, (q[:n_prompt] * scale).astype(BF16),
                                  k[:n_prompt].reshape(bp, seq, d).astype(BF16),
                                  v[:n_prompt].reshape(bp, seq, d).astype(BF16),
                                  diff_subln[i], lam_init, batch=bp, tq_total=seq, tq=min(seq, 256))
            shp = (bs, dec_seq, C_HEADS, 2, C_HD)
            qs = _apply_rope(q[n_prompt:].reshape(shp), cos, sin).reshape(n_sample, d)
            ks = _apply_rope(k[n_prompt:].reshape(shp), cos, sin).reshape(bs, dec_seq, d)
            k_all = jnp.concatenate([ks, cache_k[:, i].reshape(bs, -1, d)], axis=1).astype(BF16)
            v_all = jnp.concatenate([v[n_prompt:].reshape(bs, dec_seq, d), cache_v[:, i].reshape(bs, -1, d)],
                                    axis=1).astype(BF16)
            o_s = _diff_attention(lam, (qs * scale).astype(BF16), k_all, v_all, diff_subln[i], lam_init,
                                  batch=bs, tq_total=dec_seq, tq=min(dec_seq, 256))
            x = _proj_residual(x, jnp.concatenate([o_p, o_s], axis=0), g1, w_out_odd[i], tm=tm,
                               name="out_odd", **tok)

        qp, hb = _norm_mod_matmul(x, norm_ffn[l], sc2, sh2, peer_w_q[l], tm=tm, passes=3, emit_h=True,
                                  name="peer_q", **tok)
        t_, c1_, s2_, e2_ = _peer_select(qp, peer_sub_keys[l], tm=tm)
        x = _peer_dense(x, hb, g2, peer_u[l], peer_v[l], t_, c1_, s2_, e2_, tm=tm_dense, stages=4, **tok)

    yfin = _final_norm(x, final_norm, tm=tm)
    y_prompt = yfin[:n_prompt].reshape(bp, seq, d)
    y_sample = yfin[n_prompt:].reshape(bs, dec_seq, d)
    return (y_prompt, y_sample, jnp.stack(new_s, axis=1), jnp.stack(new_k, axis=1), jnp.stack(new_v, axis=1))
```

```python
import functools
import math

import jax
import jax.numpy as jnp
from jax import lax
from jax.experimental import pallas as pl
from jax.experimental.pallas import tpu as pltpu

D_MODEL = 1024
GRID_W = 64
ROPE_BASE = 10000.0
NORM_EPS = 1e-6
D_HALF = D_MODEL // 2
A_HEAD = 64
A_HEADS = D_HALF // A_HEAD
A_DECAY_LORA = 64
A_ICLR_LORA = 64
A_GATE_LORA = 128
GN_EPS = 64e-5
C_HEADS = 8
C_HD = D_MODEL // C_HEADS // 2
P_HEADS = 8
P_NKEYS = 128
P_EXPERTS = P_NKEYS * P_NKEYS
P_QDIM = 256
P_TOPK = 16
LORA_W = A_DECAY_LORA + A_ICLR_LORA + A_GATE_LORA
EVEN_IN = 6 * D_HALF + LORA_W

LANES = 128
SUBLANES = 8
VMEM_LIMIT = 56 * 1024 * 1024

F32 = jnp.float32
BF16 = jnp.bfloat16
NN = (((1,), (0,)), ((), ()))
NT = (((1,), (1,)), ((), ()))
NEG_INF = float("-inf")


def _cparams(*sem):
    return pltpu.CompilerParams(dimension_semantics=sem, vmem_limit_bytes=VMEM_LIMIT)


def _split2(a):
    hi = a.astype(BF16)
    lo = (a - hi.astype(F32)).astype(BF16)
    return hi, lo


def _split3(a):
    hi = a.astype(BF16)
    r = a - hi.astype(F32)
    mid = r.astype(BF16)
    lo = (r - mid.astype(F32)).astype(BF16)
    return hi, mid, lo


def _dot(a, b, dims=NN):
    return lax.dot_general(a, b, dims, preferred_element_type=F32)


def _dot3(a, b_hi, b_lo, dims=NN):
    a_hi, a_lo = _split2(a)
    return _dot(a_hi, b_hi, dims) + (_dot(a_hi, b_lo, dims) + _dot(a_lo, b_hi, dims))


def _wparts(w, passes):
    return (w.astype(BF16),) if passes == 1 else _split2(w)


def _wdot(a, parts, dims=NN):
    if len(parts) == 1:
        return _dot(a.astype(BF16), parts[0], dims)
    return _dot3(a, parts[0], parts[1], dims)


def _dot_exact01(a, ones_bf16):
    hi, mid, lo = _split3(a)
    return _dot(hi, ones_bf16) + (_dot(mid, ones_bf16) + _dot(lo, ones_bf16))


def _mod_row(i, tm, n_prompt, dec_seq):
    np_tiles = n_prompt // tm
    per = dec_seq // tm
    return jnp.where(i < np_tiles, 0, 1 + (i - np_tiles) // per)


def _ada_kernel(c_ref, whi_ref, wlo_ref, b_ref, o_ref):
    c = c_ref[...]
    s = c * jax.nn.sigmoid(c)
    o_ref[0] = _dot3(s, whi_ref[0], wlo_ref[0]) + b_ref[0]


def _adaln_all(cond, w_ada, b_ada):
    depth, d, n6 = w_ada.shape
    rows = cond.shape[0]
    tn = n6 // 4
    whi, wlo = _split2(w_ada)
    return pl.pallas_call(
        _ada_kernel,
        grid=(depth, n6 // tn),
        in_specs=[
            pl.BlockSpec((rows, d), lambda l, j: (0, 0)),
            pl.BlockSpec((1, d, tn), lambda l, j: (l, 0, j)),
            pl.BlockSpec((1, d, tn), lambda l, j: (l, 0, j)),
            pl.BlockSpec((1, 1, tn), lambda l, j: (l, 0, j)),
        ],
        out_specs=pl.BlockSpec((1, rows, tn), lambda l, j: (l, 0, j)),
        out_shape=jax.ShapeDtypeStruct((depth, rows, n6), F32),
        compiler_params=_cparams("parallel", "parallel"),
        name="adaln",
    )(cond, whi, wlo, b_ada.reshape(depth, 1, n6))


def _nmm_kernel(x_ref, g_ref, sc_ref, sh_ref, *rest, n_w):
    w_refs, o_ref, maybe_h = rest[:n_w], rest[n_w], rest[n_w + 1:]
    x = x_ref[...]
    y = x * lax.rsqrt(jnp.mean(x * x, axis=-1, keepdims=True) + NORM_EPS)
    h = y * g_ref[...] * (1.0 + sc_ref[...]) + sh_ref[...]
    o_ref[...] = _wdot(h, tuple(r[...] for r in w_refs))
    if maybe_h:
        maybe_h[0][...] = h.T.astype(BF16)


def _norm_mod_matmul(x, gnorm, scale, shift, w, *, tm, n_prompt, dec_seq, passes, emit_h=False, name):
    n, d = x.shape
    nout = w.shape[1]
    tn = nout if emit_h else nout // 2
    wp = _wparts(w, passes)
    mod_map = lambda j, i: (_mod_row(i, tm, n_prompt, dec_seq), 0, 0)
    out_shape = [jax.ShapeDtypeStruct((n, nout), F32)]
    out_specs = [pl.BlockSpec((tm, tn), lambda j, i: (i, j))]
    if emit_h:
        out_shape.append(jax.ShapeDtypeStruct((d, n), BF16))
        out_specs.append(pl.BlockSpec((d, tm), lambda j, i: (0, i)))
    res = pl.pallas_call(
        functools.partial(_nmm_kernel, n_w=len(wp)),
        grid=(nout // tn, n // tm),
        in_specs=[
            pl.BlockSpec((tm, d), lambda j, i: (i, 0)),
            pl.BlockSpec((1, d), lambda j, i: (0, 0)),
            pl.BlockSpec((None, 1, d), mod_map),
            pl.BlockSpec((None, 1, d), mod_map),
        ] + [pl.BlockSpec((d, tn), lambda j, i: (0, j))] * len(wp),
        out_specs=out_specs,
        out_shape=out_shape,
        compiler_params=_cparams("arbitrary", "arbitrary"),
        name=name,
    )(x, gnorm.reshape(1, d), scale, shift, *wp)
    return res if emit_h else res[0]


def _proj_res_kernel(x_ref, a_ref, gate_ref, w_ref, o_ref):
    o_ref[...] = x_ref[...] + gate_ref[...] * _dot(a_ref[...].astype(BF16), w_ref[...])


def _proj_residual(x, a, gate, w, *, tm, n_prompt, dec_seq, name):
    n, d = x.shape
    k = a.shape[1]
    mod_map = lambda i: (_mod_row(i, tm, n_prompt, dec_seq), 0, 0)
    return pl.pallas_call(
        _proj_res_kernel,
        grid=(n // tm,),
        in_specs=[
            pl.BlockSpec((tm, d), lambda i: (i, 0)),
            pl.BlockSpec((tm, k), lambda i: (i, 0)),
            pl.BlockSpec((None, 1, d), mod_map),
            pl.BlockSpec((k, d), lambda i: (0, 0)),
        ],
        out_specs=pl.BlockSpec((tm, d), lambda i: (i, 0)),
        out_shape=jax.ShapeDtypeStruct((n, d), F32),
        compiler_params=_cparams("parallel"),
        name=name,
    )(x, a, gate, w.astype(BF16))


def _rwkv_prep_kernel(r_ref, k_ref, v_ref, cg_ref, hc_ref, lora_ref,
                      w0_ref, wuph_ref, wupl_ref, a0_ref, auph_ref, aupl_ref, guph_ref, gupl_ref,
                      kk_ref_p, ka_ref_p, rk_ref_p, ones_ref,
                      kk_out, w_out, kd_out, kka_out, bonus_out, g_out, z_out):
    r = r_ref[...]
    k = k_ref[...]
    v = v_ref[...]
    lora = lora_ref[...]
    ones = ones_ref[...]
    wd = lora[:, :A_DECAY_LORA]
    ad = lora[:, A_DECAY_LORA:A_DECAY_LORA + A_ICLR_LORA]
    gd = lora[:, A_DECAY_LORA + A_ICLR_LORA:]

    kk = k * kk_ref_p[...]
    nrm = jnp.sqrt(_dot_exact01(kk * kk, ones))
    kk = kk / jnp.maximum(nrm, 1e-12)
    kk_out[...] = kk

    tw = jnp.tanh(wd)
    bonus = None
    for d in range(2):
        w_pre = w0_ref[d] + _dot3(tw, wuph_ref[d], wupl_ref[d])
        w_out[d] = jnp.exp(-(math.exp(-0.5) * jax.nn.sigmoid(w_pre)))
        a = jax.nn.sigmoid(a0_ref[d] + _dot3(ad, auph_ref[d], aupl_ref[d]))
        kd = k * (1.0 + (a - 1.0) * ka_ref_p[...])
        kd_out[d] = kd
        kka_out[d] = kk * a
        b = _dot_exact01(r * kd * rk_ref_p[...], ones) * v
        bonus = b if bonus is None else bonus + b
    bonus_out[...] = bonus
    g_out[...] = _dot3(jax.nn.sigmoid(gd), guph_ref[...], gupl_ref[...])
    z_out[...] = cg_ref[...] * hc_ref[...]


def _head_ones():
    idx = jnp.arange(D_HALF) // A_HEAD
    return (idx[:, None] == idx[None, :]).astype(BF16)


def _rwkv_prep(y, w0, w_up, a0, a_up, g_up, k_k, k_a, r_k, *, tm):
    n = y.shape[0]
    h = D_HALF
    wuph, wupl = _split2(w_up)
    auph, aupl = _split2(a_up)
    guph, gupl = _split2(g_up)
    col = lambda c: pl.BlockSpec((tm, h), lambda i: (i, c))
    full2 = lambda s: pl.BlockSpec(s, lambda i: (0, 0))
    full3 = lambda s: pl.BlockSpec(s, lambda i: (0, 0, 0))
    tok = pl.BlockSpec((tm, h), lambda i: (i, 0))
    tok2 = pl.BlockSpec((2, tm, h), lambda i: (0, i, 0))
    sds = lambda *s: jax.ShapeDtypeStruct(s, F32)
    return pl.pallas_call(
        _rwkv_prep_kernel,
        grid=(n // tm,),
        in_specs=[col(0), col(1), col(2), col(4), col(5),
                  pl.BlockSpec((tm, LORA_W), lambda i: (i, 6 * h // LORA_W)),
                  full3((2, 1, h)), full3((2, A_DECAY_LORA, h)), full3((2, A_DECAY_LORA, h)),
                  full3((2, 1, h)), full3((2, A_ICLR_LORA, h)), full3((2, A_ICLR_LORA, h)),
                  full2((A_GATE_LORA, h)), full2((A_GATE_LORA, h)),
                  full2((1, h)), full2((1, h)), full2((1, h)), full2((h, h))],
        out_specs=[tok, tok2, tok2, tok2, tok, tok, tok],
        out_shape=[sds(n, h), sds(2, n, h), sds(2, n, h), sds(2, n, h), sds(n, h), sds(n, h), sds(n, h)],
        compiler_params=_cparams("parallel"),
        name="rwkv_prep",
    )(y, y, y, y, y, y,
      w0.reshape(2, 1, h), wuph, wupl, a0.reshape(2, 1, h), auph, aupl, guph, gupl,
      k_k.reshape(1, h), k_a.reshape(1, h), r_k.reshape(1, h), _head_ones())


def _scan_kernel(r_ref, kk_ref, v_ref, w_ref, kd_ref, kka_ref, s0_ref, o_ref, sf_ref, s_scr, *, tc, g):
    d = pl.program_id(0)
    c = pl.program_id(2)
    nk = A_HEAD
    parts = 4

    @pl.when(c == 0)
    def _():
        s_scr[...] = s0_ref[...]

    def row(ref, t, k):
        return jnp.broadcast_to(ref[t, pl.ds(k, 1), :], (g, LANES))

    def tree(xs):
        while len(xs) > 1:
            xs = [xs[i] + xs[i + 1] for i in range(0, len(xs) - 1, 2)] + ([xs[-1]] if len(xs) % 2 else [])
        return xs[0]

    def step(i, carry):
        t = i + d * (tc - 1 - 2 * i)
        acc = [None] * parts
        for k in range(nk):
            p = s_scr[k] * row(kk_ref, t, k)
            acc[k % parts] = p if acc[k % parts] is None else acc[k % parts] + p
        sa = tree(acc)
        vv = v_ref[t]
        acc = [None] * parts
        for k in range(nk):
            sn = s_scr[k] * row(w_ref, t, k) - sa * row(kka_ref, t, k) + vv * row(kd_ref, t, k)
            s_scr[k] = sn
            p = sn * row(r_ref, t, k)
            acc[k % parts] = p if acc[k % parts] is None else acc[k % parts] + p
        o_ref[t] = tree(acc)
        return carry

    lax.fori_loop(0, tc, step, 0)

    @pl.when(c == pl.num_programs(2) - 1)
    def _():
        sf_ref[...] = s_scr[...]


def _wkv_scan(r, kk, v, w, kd, kka, s0, *, tc):
    lg, t, nk, _ = r.shape
    g = v.shape[2]
    nt = t // tc
    tmap = lambda d, l, c: c + d * (nt - 1 - 2 * c)
    shared = lambda rows: pl.BlockSpec((None, tc, rows, LANES), lambda d, l, c: (l, tmap(d, l, c), 0, 0))
    perdir = pl.BlockSpec((None, None, tc, nk, LANES), lambda d, l, c: (d, l, tmap(d, l, c), 0, 0))
    state = pl.BlockSpec((None, None, nk, g, LANES), lambda d, l, c: (d, l, 0, 0, 0))
    return pl.pallas_call(
        functools.partial(_scan_kernel, tc=tc, g=g),
        grid=(2, lg, nt),
        in_specs=[shared(nk), shared(nk), shared(g), perdir, perdir, perdir, state],
        out_specs=[pl.BlockSpec((None, None, tc, g, LANES), lambda d, l, c: (d, l, tmap(d, l, c), 0, 0)), state],
        out_shape=[jax.ShapeDtypeStruct((2, lg, t, g, LANES), F32),
                   jax.ShapeDtypeStruct((2, lg, nk, g, LANES), F32)],
        scratch_shapes=[pltpu.VMEM((nk, g, LANES), F32)],
        compiler_params=_cparams("parallel", "parallel", "arbitrary"),
        name="wkv_scan",
    )(r, kk, v, w, kd, kka, s0)


def _chain_layout(batch):
    chains = batch * A_HEADS
    if chains >= LANES:
        assert chains % LANES == 0
        return chains // LANES, 1
    assert LANES % chains == 0 and A_HEAD % (LANES // chains) == 0
    return 1, LANES // chains


def _to_lanes_k(x, batch, t, lg, vq):
    lead = x.shape[:-2]
    nl = len(lead)
    chains = (batch // lg) * A_HEADS
    x = x.reshape(lead + (lg, batch // lg, t, A_HEADS, A_HEAD))
    perm = tuple(range(nl)) + tuple(nl + p for p in (0, 2, 4, 1, 3))
    x = jnp.transpose(x, perm).reshape(lead + (lg, t, A_HEAD, 1, chains))
    x = jnp.broadcast_to(x, lead + (lg, t, A_HEAD, vq, chains))
    return x.reshape(lead + (lg, t, A_HEAD, vq * chains))


def _to_lanes_v(x, batch, t, lg, vq):
    gsz = A_HEAD // vq
    x = x.reshape(lg, batch // lg, t, A_HEADS, vq, gsz)
    x = jnp.transpose(x, (0, 2, 5, 4, 1, 3))
    return x.reshape(lg, t, gsz, vq * (batch // lg) * A_HEADS)


def _from_lanes_v(o, batch, t, lg, vq):
    lead = o.shape[:-4]
    gsz = A_HEAD // vq
    o = o.reshape(lead + (lg, t, gsz, vq, batch // lg, A_HEADS))
    nl = len(lead)
    perm = tuple(range(nl)) + tuple(nl + p for p in (0, 4, 1, 5, 3, 2))
    return jnp.transpose(o, perm).reshape(lead + (batch * t, D_HALF))


def _state_to_lanes(s, lg, vq):
    batch = s.shape[0]
    gsz = A_HEAD // vq
    s = s.reshape(lg, batch // lg, 2, A_HEADS, vq, gsz, A_HEAD)
    s = jnp.transpose(s, (2, 0, 6, 5, 4, 1, 3))
    return s.reshape(2, lg, A_HEAD, gsz, LANES)


def _state_from_lanes(s, batch, lg, vq):
    gsz = A_HEAD // vq
    s = s.reshape(2, lg, A_HEAD, gsz, vq, batch // lg, A_HEADS)
    s = jnp.transpose(s, (1, 5, 0, 6, 4, 3, 2))
    return s.reshape(batch, 2, A_HEADS, A_HEAD, A_HEAD)


def _even_post_kernel(x_ref, o_ref_in, bonus_ref, g_ref, bg_ref, z_ref, zprev_ref, znext_ref,
                      lng_ref, lnb_ref, cw_ref, ones_ref, gate_ref, w_ref, out_ref,
                      *, tm, np_tiles, seq_tiles, dec_tiles):
    i = pl.program_id(0)
    ones = ones_ref[...]
    inv = 1.0 / A_HEAD
    o = o_ref_in[0] + o_ref_in[1]
    mu = _dot_exact01(o, ones) * inv
    xc = o - mu
    var = _dot_exact01(xc * xc, ones) * inv
    on = xc * lax.rsqrt(var + GN_EPS) * lng_ref[...] + lnb_ref[...]
    ya = (on + bonus_ref[...]) * g_ref[...]

    pos = jnp.where(i < np_tiles, i % seq_tiles, (i - np_tiles) % dec_tiles)
    last = jnp.where(i < np_tiles, seq_tiles - 1, dec_tiles - 1)
    z = z_ref[...]
    rows = lax.broadcasted_iota(jnp.int32, z.shape, 0)
    prev_row = jnp.where(pos == 0, 0.0, zprev_ref[pl.ds(SUBLANES - 1, 1), :])
    next_row = jnp.where(pos == last, 0.0, znext_ref[pl.ds(0, 1), :])
    zp = jnp.where(rows == 0, prev_row, pltpu.roll(z, 1, axis=0))
    zn = jnp.where(rows == tm - 1, next_row, pltpu.roll(z, tm - 1, axis=0))
    yb = bg_ref[...] * (zp * cw_ref[pl.ds(0, 1), :] + z * cw_ref[pl.ds(1, 1), :] + zn * cw_ref[pl.ds(2, 1), :])

    h = D_HALF
    y = _dot(ya.astype(BF16), w_ref[pl.ds(0, h), :]) + _dot(yb.astype(BF16), w_ref[pl.ds(h, h), :])
    out_ref[...] = x_ref[...] + gate_ref[...] * y


def _even_post(x, o2, bonus, g, y, z, ln_g, ln_b, conv_w, gate, w_out, *, tm, n_prompt, seq, dec_seq):
    n, d = x.shape
    h = D_HALF
    nblk8 = n // SUBLANES
    per8 = tm // SUBLANES
    tok = pl.BlockSpec((tm, h), lambda i: (i, 0))
    full2 = lambda s: pl.BlockSpec(s, lambda i: (0, 0))
    mod_map = lambda i: (_mod_row(i, tm, n_prompt, dec_seq), 0, 0)
    return pl.pallas_call(
        functools.partial(_even_post_kernel, tm=tm, np_tiles=n_prompt // tm, seq_tiles=seq // tm,
                          dec_tiles=dec_seq // tm),
        grid=(n // tm,),
        in_specs=[
            pl.BlockSpec((tm, d), lambda i: (i, 0)),
            pl.BlockSpec((2, tm, h), lambda i: (0, i, 0)),
            tok, tok,
            pl.BlockSpec((tm, h), lambda i: (i, 3)),
            tok,
            pl.BlockSpec((SUBLANES, h), lambda i: (jnp.maximum(i * per8 - 1, 0), 0)),
            pl.BlockSpec((SUBLANES, h), lambda i: (jnp.minimum((i + 1) * per8, nblk8 - 1), 0)),
            full2((1, h)), full2((1, h)), full2((SUBLANES, h)), full2((h, h)),
            pl.BlockSpec((None, 1, d), mod_map),
            full2((d, d)),
        ],
        out_specs=pl.BlockSpec((tm, d), lambda i: (i, 0)),
        out_shape=jax.ShapeDtypeStruct((n, d), F32),
        compiler_params=_cparams("parallel"),
        name="even_post",
    )(x, o2, bonus, g, y, z, z, z, ln_g.reshape(1, h), ln_b.reshape(1, h),
      jnp.pad(conv_w, ((0, SUBLANES - conv_w.shape[0]), (0, 0))), _head_ones(), gate, w_out.astype(BF16))


def _attn_kernel(lam_ref, q_ref, k_ref, v_ref, subln_ref, o_ref, *, out_scale):
    lam = lam_ref[0]
    hd2 = 2 * C_HD
    for h in range(C_HEADS):
        ps = []
        for j in range(2):
            lo = h * hd2 + j * C_HD
            s = _dot(q_ref[:, lo:lo + C_HD], k_ref[:, lo:lo + C_HD], NT)
            e = jnp.exp(s - jnp.max(s, axis=-1, keepdims=True))
            ps.append((e, 1.0 / jnp.sum(e, axis=-1, keepdims=True)))
        attn = ps[0][0] * ps[0][1] - ps[1][0] * (lam * ps[1][1])
        oh = _dot(attn.astype(BF16), v_ref[:, h * hd2:(h + 1) * hd2])
        y = oh * lax.rsqrt(jnp.mean(oh * oh, axis=-1, keepdims=True) + NORM_EPS)
        o_ref[:, h * hd2:(h + 1) * hd2] = y * subln_ref[...] * out_scale


def _diff_attention(lam, q, k, v, subln, lam_init, *, batch, tq_total, tq):
    d = q.shape[1]
    tk = k.shape[1]
    nq = tq_total // tq
    return pl.pallas_call(
        functools.partial(_attn_kernel, out_scale=1.0 - lam_init),
        grid=(batch, nq),
        in_specs=[
            pl.BlockSpec(memory_space=pltpu.SMEM),
            pl.BlockSpec((tq, d), lambda b, i: (b * nq + i, 0)),
            pl.BlockSpec((None, tk, d), lambda b, i: (b, 0, 0)),
            pl.BlockSpec((None, tk, d), lambda b, i: (b, 0, 0)),
            pl.BlockSpec((1, 2 * C_HD), lambda b, i: (0, 0)),
        ],
        out_specs=pl.BlockSpec((tq, d), lambda b, i: (b * nq + i, 0)),
        out_shape=jax.ShapeDtypeStruct((batch * tq_total, d), F32),
        compiler_params=_cparams("parallel", "parallel"),
        name="diff_attn",
    )(lam.reshape(1), q, k, v, subln.reshape(1, 2 * C_HD))


def _axial_rope_tables(n_tok):
    rows = n_tok // GRID_W
    row = jnp.repeat(jnp.arange(rows, dtype=F32), GRID_W)
    col = jnp.tile(jnp.arange(GRID_W, dtype=F32), rows)
    nf = C_HD // 4
    inv = ROPE_BASE ** (-jnp.arange(nf, dtype=F32) / nf)
    ar = row[:, None] * inv[None, :]
    ac = col[:, None] * inv[None, :]
    ang = jnp.concatenate([ar, ar, ac, ac], axis=-1)
    return jnp.cos(ang), jnp.sin(ang)


def _apply_rope(x, cos, sin):
    h = x.shape[-1] // 2
    q = h // 2
    xr = jnp.concatenate([-x[..., q:h], x[..., :q], -x[..., h + q:], x[..., h:h + q]], axis=-1)
    return x * cos[None, :, None, None, :] + xr * sin[None, :, None, None, :]


N_TOP = P_TOPK + 1
CAND_PAIRS = [(i, j) for i in range(N_TOP) for j in range(N_TOP) if (i + 1) * (j + 1) <= N_TOP]
CAND_ROWS = -(-len(CAND_PAIRS) // SUBLANES) * SUBLANES
TOP_ROWS = -(-N_TOP // SUBLANES) * SUBLANES


def _take_top(x, n, emit, want_rank=False):
    rank = jnp.full(x.shape, float(n), F32) if want_rank else None
    for it in range(n):
        m = jnp.max(x, axis=0, keepdims=True)
        emit(it, m)
        hit = x == m
        if want_rank:
            rank = jnp.where(hit, float(it), rank)
        if it + 1 < n:
            x = jnp.where(hit, NEG_INF, x)
    return rank


def _pack_bf16_rows(x):
    return pltpu.bitcast(x.astype(BF16), jnp.int32)


def _peer_select_kernel(q_ref, skh_ref, skl_ref, l_out, c1_out, r2_out, e2_out, top_scr, cand_scr):
    half = P_QDIM // 2
    for h in range(P_HEADS):
        sT = []
        for j in range(2):
            hj = 2 * h + j
            qs = q_ref[:, hj * half:(hj + 1) * half]
            q_hi, q_lo = _split2(qs)
            s = _dot(skh_ref[hj], q_hi, NT) + (_dot(skh_ref[hj], q_lo, NT) + _dot(skl_ref[hj], q_hi, NT))
            sT.append(s)

            def emit(it, m, j=j):
                top_scr[j, pl.ds(it, 1), :] = m
            rank = _take_top(s, N_TOP, emit, want_rank=(j == 1))

        cand_scr[...] = jnp.full(cand_scr.shape, NEG_INF, F32)
        for c, (i, j) in enumerate(CAND_PAIRS):
            cand_scr[pl.ds(c, 1), :] = top_scr[0, pl.ds(i, 1), :] + top_scr[1, pl.ds(j, 1), :]
        a1 = top_scr[0, pl.ds(0, 1), :]
        b1 = top_scr[1, pl.ds(0, 1), :]
        mx = a1 + b1
        st = {"z": None, "c16": None, "c17": None}

        def emit_c(it, m):
            if it < P_TOPK:
                e = jnp.exp(m - mx)
                st["z"] = e if st["z"] is None else st["z"] + e
            if it == P_TOPK - 1:
                st["c16"] = m
            if it == P_TOPK:
                st["c17"] = m
        _take_top(cand_scr[...], N_TOP, emit_c)
        tau = 0.5 * (st["c16"] + st["c17"])
        thr = tau - sT[0]
        cnt = jnp.zeros(thr.shape, F32)
        for k in range(N_TOP):
            cnt = cnt + jnp.where(top_scr[1, pl.ds(k, 1), :] >= thr, 1.0, 0.0)
        l_out[h] = cnt
        c1_out[h] = jnp.exp(sT[0] - a1) / st["z"]
        e2 = jnp.exp(sT[1] - b1)
        for g in range(e2.shape[1] // LANES):
            r2_out[h, g] = _pack_bf16_rows(rank[:, g * LANES:(g + 1) * LANES])
            e2_out[h, g] = _pack_bf16_rows(e2[:, g * LANES:(g + 1) * LANES])


def _peer_select(q, sub_keys, *, tm):
    n = q.shape[0]
    sk = sub_keys.reshape(2 * P_HEADS, P_NKEYS, P_QDIM // 2)
    skh, skl = _split2(sk)
    out = pl.BlockSpec((P_HEADS, P_NKEYS, tm), lambda i: (0, 0, i))
    sds = jax.ShapeDtypeStruct((P_HEADS, P_NKEYS, n), F32)
    out_g = pl.BlockSpec((P_HEADS, tm // LANES, P_NKEYS // 2, LANES), lambda i: (0, i, 0, 0))
    sds_g = jax.ShapeDtypeStruct((P_HEADS, n // LANES, P_NKEYS // 2, LANES), jnp.int32)
    return pl.pallas_call(
        _peer_select_kernel,
        grid=(n // tm,),
        in_specs=[pl.BlockSpec((tm, P_HEADS * P_QDIM), lambda i: (i, 0)),
                  pl.BlockSpec(sk.shape, lambda i: (0, 0, 0)),
                  pl.BlockSpec(sk.shape, lambda i: (0, 0, 0))],
        out_specs=[out, out, out_g, out_g],
        out_shape=[sds, sds, sds_g, sds_g],
        scratch_shapes=[pltpu.VMEM((2, TOP_ROWS, tm), F32), pltpu.VMEM((CAND_ROWS, tm), F32)],
        compiler_params=_cparams("parallel"),
        name="peer_select",
    )(q, skh, skl)


PAIR = 2
PAIR_ROWS = PAIR * P_NKEYS
I2_HALF = P_NKEYS // 2


def _gelu(x):
    return 0.5 * x * (1.0 + lax.erf(x * (1.0 / math.sqrt(2.0))))


def _peer_dense_kernel(hbt_ref, x_ref, gate_ref, ufirst_ref, us_ref, vts_ref, vlast_ref,
                       l_ref, c1_ref, r2_ref, e2_ref, o_ref, acc_ref, a_scr, act_scr, *, stages, tm):
    c = pl.program_id(1)

    def put_act(slot, act):
        for lg in range(tm // LANES):
            act_scr[slot, lg] = act[:, lg * LANES:(lg + 1) * LANES]

    @pl.when(c == 0)
    def _():
        acc_ref[...] = jnp.zeros(acc_ref.shape, F32)
        a_scr[stages - 1] = jnp.zeros(a_scr.shape[1:], BF16)
        put_act(0, _gelu(_dot(ufirst_ref[...], hbt_ref[...])))

    def stage(p, carry):
        prev = (p + stages - 1) % stages
        nxt = (p + 1) % stages
        acc_ref[...] += _dot(vts_ref[p], a_scr[prev])
        for lg in range(tm // LANES):
            ls = slice(lg * LANES, (lg + 1) * LANES)
            for half in range(P_NKEYS // I2_HALF):
                words = slice(half * I2_HALF // 2, (half + 1) * I2_HALF // 2)
                w = [None] * PAIR
                for h in range(P_HEADS):
                    r2 = pltpu.bitcast(r2_ref[h, lg, words, :], BF16)
                    e2 = pltpu.bitcast(e2_ref[h, lg, words, :], BF16)
                    for j in range(PAIR):
                        row = pl.ds(PAIR * h + j, 1)
                        cnt = jnp.broadcast_to(l_ref[p, row, ls], (I2_HALF, LANES)).astype(BF16)
                        c1 = jnp.broadcast_to(c1_ref[p, row, ls], (I2_HALF, LANES)).astype(BF16)
                        term = jnp.where(r2 < cnt, e2 * c1, jnp.zeros_like(e2))
                        w[j] = term if w[j] is None else w[j] + term
                for j in range(PAIR):
                    rs = slice(j * P_NKEYS + half * I2_HALF, j * P_NKEYS + (half + 1) * I2_HALF)
                    a_scr[p, rs, ls] = w[j] * act_scr[p, lg, rs, :].astype(BF16)
        r0 = pl.multiple_of(p * PAIR_ROWS, PAIR_ROWS)
        put_act(nxt, _gelu(_dot(us_ref[pl.ds(r0, PAIR_ROWS), :], hbt_ref[...])))
        return carry

    lax.fori_loop(0, stages, stage, 0)

    @pl.when(c == pl.num_programs(1) - 1)
    def _():
        total = acc_ref[...] + _dot(vlast_ref[0], a_scr[stages - 1])
        o_ref[...] = x_ref[...] + gate_ref[...] * total.T


def _peer_dense(x, hbt, gate, u, v, cnt, c1, r2, e2, *, tm, stages, n_prompt, dec_seq):
    n, d = x.shape
    ec = stages * PAIR_ROWS
    n_pairs = P_EXPERTS // PAIR_ROWS
    u_bf = u.astype(BF16)
    u_shift = jnp.concatenate([u_bf[PAIR_ROWS:], jnp.zeros((PAIR_ROWS, d), BF16)], axis=0)
    vt = jnp.transpose(v.astype(BF16).reshape(n_pairs, PAIR_ROWS, d), (0, 2, 1))
    vt_shift = jnp.concatenate([jnp.zeros((1, d, PAIR_ROWS), BF16), vt], axis=0)
    mod_map = lambda i, c: (_mod_row(i, tm, n_prompt, dec_seq), 0, 0)
    def pair_major(a):
        a = a.reshape(P_HEADS, n_pairs, PAIR, n)
        return jnp.transpose(a, (1, 0, 2, 3)).reshape(n_pairs, P_HEADS * PAIR, n)
    cnt, c1 = pair_major(cnt), pair_major(c1)
    small = pl.BlockSpec((stages, P_HEADS * PAIR, tm), lambda i, c: (c, 0, i))
    big = pl.BlockSpec((P_HEADS, tm // LANES, P_NKEYS // 2, LANES), lambda i, c: (0, i, 0, 0))
    return pl.pallas_call(
        functools.partial(_peer_dense_kernel, stages=stages, tm=tm),
        grid=(n // tm, P_EXPERTS // ec),
        in_specs=[
            pl.BlockSpec((d, tm), lambda i, c: (0, i)),
            pl.BlockSpec((tm, d), lambda i, c: (i, 0)),
            pl.BlockSpec((None, 1, d), mod_map),
            pl.BlockSpec((PAIR_ROWS, d), lambda i, c: (0, 0)),
            pl.BlockSpec((ec, d), lambda i, c: (c, 0)),
            pl.BlockSpec((stages, d, PAIR_ROWS), lambda i, c: (c, 0, 0)),
            pl.BlockSpec((1, d, PAIR_ROWS), lambda i, c: (n_pairs, 0, 0)),
            small, small, big, big,
        ],
        out_specs=pl.BlockSpec((tm, d), lambda i, c: (i, 0)),
        out_shape=jax.ShapeDtypeStruct((n, d), F32),
        scratch_shapes=[pltpu.VMEM((d, tm), F32), pltpu.VMEM((stages, PAIR_ROWS, tm), BF16),
                        pltpu.VMEM((stages, tm // LANES, PAIR_ROWS, LANES), F32)],
        compiler_params=_cparams("parallel", "arbitrary"),
        name="peer_dense",
    )(hbt, x, gate, u_bf, u_shift, vt_shift, vt_shift, cnt, c1, r2, e2)


def _final_norm_kernel(x_ref, g_ref, o_ref):
    x = x_ref[...]
    o_ref[...] = x * lax.rsqrt(jnp.mean(x * x, axis=-1, keepdims=True) + NORM_EPS) * g_ref[...]


def _final_norm(x, g, *, tm):
    n, d = x.shape
    return pl.pallas_call(
        _final_norm_kernel,
        grid=(n // tm,),
        in_specs=[pl.BlockSpec((tm, d), lambda i: (i, 0)), pl.BlockSpec((1, d), lambda i: (0, 0))],
        out_specs=pl.BlockSpec((tm, d), lambda i: (i, 0)),
        out_shape=jax.ShapeDtypeStruct((n, d), F32),
        compiler_params=_cparams("parallel"),
        name="final_norm",
    )(x, g.reshape(1, d))


def _tiles(seq, dec_seq):
    tm = math.gcd(math.gcd(seq, dec_seq), 256)
    tm_dense = math.gcd(math.gcd(seq, dec_seq), 512)
    return tm, tm_dense


def kernel(x_prompt, x_sample, c, state_rwkv, cache_k, cache_v, c_ctx, w_ada, b_ada, norm_mix, norm_ffn, w_in_even, rwkv_w0, rwkv_w_up, rwkv_a0, rwkv_a_up, rwkv_g_up, rwkv_k_k, rwkv_k_a, rwkv_r_k, rwkv_ln_g, rwkv_ln_b, conv_w, w_out_even, w_in_odd, diff_lam, diff_subln, w_out_odd, peer_w_q, peer_sub_keys, peer_u, peer_v, final_norm):
    bp, seq, d = x_prompt.shape
    bs, dec_seq, _ = x_sample.shape
    depth = w_ada.shape[0]
    n_prompt = bp * seq
    n_sample = bs * dec_seq
    tm, tm_dense = _tiles(seq, dec_seq)
    tm_dense = math.gcd(math.gcd(n_prompt, dec_seq), 512)
    tok = dict(n_prompt=n_prompt, dec_seq=dec_seq)

    x = jnp.concatenate([x_prompt.reshape(n_prompt, d), x_sample.reshape(n_sample, d)], axis=0)

    rows = 1 + bs
    rows_pad = -(-rows // SUBLANES) * SUBLANES
    cond = jnp.zeros((rows_pad, d), F32).at[0].set(c_ctx).at[1:rows].set(c)
    mod = _adaln_all(cond, w_ada, b_ada)[:, :rows].reshape(depth, rows, 6, 1, d)
    mods = [[mod[l, :, m] for m in range(6)] for l in range(depth)]

    h = D_HALF
    perm = jnp.concatenate([jnp.arange(0, 3 * h), jnp.arange(3 * h + LORA_W, 6 * h + LORA_W),
                            jnp.arange(3 * h, 3 * h + LORA_W)])

    lg_p, vq_p = _chain_layout(bp)
    lg_s, vq_s = _chain_layout(bs)
    cos, sin = _axial_rope_tables(dec_seq)
    scale = C_HD ** -0.5

    new_s, new_k, new_v = [], [], []
    for l in range(depth):
        i = l // 2
        sh1, sc1, g1, sh2, sc2, g2 = mods[l]
        if l % 2 == 0:
            y = _norm_mod_matmul(x, norm_mix[l], sc1, sh1, w_in_even[i][:, perm], tm=tm, passes=1, name="in_even", **tok)
            kk, w, kd, kka, bonus, g, z = _rwkv_prep(
                y, rwkv_w0[i], rwkv_w_up[i], rwkv_a0[i], rwkv_a_up[i], rwkv_g_up[i],
                rwkv_k_k[i], rwkv_k_a[i], rwkv_r_k[i].reshape(-1), tm=tm)
            r = y[:, 0:h]
            v = y[:, 2 * h:3 * h]
            outs = []
            for (lo, n_tok, batch, t, lg, vq, s0) in (
                    (0, n_prompt, bp, seq, lg_p, vq_p, None),
                    (n_prompt, n_sample, bs, dec_seq, lg_s, vq_s, state_rwkv[:, i])):
                sl = slice(lo, lo + n_tok)
                lk = lambda a: _to_lanes_k(a, batch, t, lg, vq)
                gsz = A_HEAD // vq
                s0l = (jnp.zeros((2, lg, A_HEAD, gsz, LANES), F32) if s0 is None
                       else _state_to_lanes(s0, lg, vq))
                o, sf = _wkv_scan(lk(r[sl]), lk(kk[sl]), _to_lanes_v(v[sl], batch, t, lg, vq),
                                  lk(w[:, sl]), lk(kd[:, sl]), lk(kka[:, sl]), s0l, tc=min(16, t))
                outs.append(_from_lanes_v(o, batch, t, lg, vq))
                if s0 is None:
                    new_s.append(_state_from_lanes(sf, batch, lg, vq))
            o2 = jnp.concatenate(outs, axis=1)
            x = _even_post(x, o2, bonus, g, y, z, rwkv_ln_g[i], rwkv_ln_b[i], conv_w[i], g1, w_out_even[i],
                           tm=tm, n_prompt=n_prompt, seq=seq, dec_seq=dec_seq)
        else:
            lam_init = 0.8 - 0.6 * math.exp(-0.3 * l)
            lp = diff_lam[i].astype(F32)
            lam = jnp.exp(jnp.sum(lp[0] * lp[1])) - jnp.exp(jnp.sum(lp[2] * lp[3])) + lam_init
            y = _norm_mod_matmul(x, norm_mix[l], sc1, sh1, w_in_odd[i], tm=tm, passes=1, name="in_odd", **tok)
            q, k, v = y[:, 0:d], y[:, d:2 * d], y[:, 2 * d:3 * d]
            new_k.append(k[:n_prompt].reshape(bp, seq, C_HEADS, 2, C_HD))
            new_v.append(v[:n_prompt].reshape(bp, seq, C_HEADS, 2 * C_HD))
            o_p = _diff_attention(lam, (q[:n_prompt] * scale).astype(BF16),
                                  k[:n_prompt].reshape(bp, seq, d).astype(BF16),
                                  v[:n_prompt].reshape(bp, seq, d).astype(BF16),
                                  diff_subln[i], lam_init, batch=bp, tq_total=seq, tq=min(seq, 256))
            shp = (bs, dec_seq, C_HEADS, 2, C_HD)
            qs = _apply_rope(q[n_prompt:].reshape(shp), cos, sin).reshape(n_sample, d)
            ks = _apply_rope(k[n_prompt:].reshape(shp), cos, sin).reshape(bs, dec_seq, d)
            k_all = jnp.concatenate([ks, cache_k[:, i].reshape(bs, -1, d)], axis=1).astype(BF16)
            v_all = jnp.concatenate([v[n_prompt:].reshape(bs, dec_seq, d), cache_v[:, i].reshape(bs, -1, d)],
                                    axis=1).astype(BF16)
            o_s = _diff_attention(lam, (qs * scale).astype(BF16), k_all, v_all, diff_subln[i], lam_init,
                                  batch=bs, tq_total=dec_seq, tq=min(dec_seq, 256))
            x = _proj_residual(x, jnp.concatenate([o_p, o_s], axis=0), g1, w_out_odd[i], tm=tm,
                               name="out_odd", **tok)

        qp, hbt = _norm_mod_matmul(x, norm_ffn[l], sc2, sh2, peer_w_q[l], tm=tm, passes=3, emit_h=True,
                                  name="peer_q", **tok)
        cnt_, c1_, r2_, e2_ = _peer_select(qp, peer_sub_keys[l], tm=tm)
        x = _peer_dense(x, hbt, g2, peer_u[l], peer_v[l], cnt_, c1_, r2_, e2_, tm=tm_dense, stages=4, **tok)

    yfin = _final_norm(x, final_norm, tm=tm)
    y_prompt = yfin[:n_prompt].reshape(bp, seq, d)
    y_sample = yfin[n_prompt:].reshape(bs, dec_seq, d)
    return (y_prompt, y_sample, jnp.stack(new_s, axis=1), jnp.stack(new_k, axis=1), jnp.stack(new_v, axis=1))
```

```python
import functools
import math

import jax
import jax.numpy as jnp
from jax import lax
from jax.experimental import pallas as pl
from jax.experimental.pallas import tpu as pltpu

D_MODEL = 1024
GRID_W = 64
ROPE_BASE = 10000.0
NORM_EPS = 1e-6
D_HALF = D_MODEL // 2
A_HEAD = 64
A_HEADS = D_HALF // A_HEAD
A_DECAY_LORA = 64
A_ICLR_LORA = 64
A_GATE_LORA = 128
GN_EPS = 64e-5
C_HEADS = 8
C_HD = D_MODEL // C_HEADS // 2
P_HEADS = 8
P_NKEYS = 128
P_EXPERTS = P_NKEYS * P_NKEYS
P_QDIM = 256
P_TOPK = 16
LORA_W = A_DECAY_LORA + A_ICLR_LORA + A_GATE_LORA
EVEN_IN = 6 * D_HALF + LORA_W

LANES = 128
SUBLANES = 8
VMEM_LIMIT = 56 * 1024 * 1024

F32 = jnp.float32
BF16 = jnp.bfloat16
NN = (((1,), (0,)), ((), ()))
NT = (((1,), (1,)), ((), ()))
NEG_INF = float("-inf")


def _cparams(*sem):
    return pltpu.CompilerParams(dimension_semantics=sem, vmem_limit_bytes=VMEM_LIMIT)


def _split2(a):
    hi = a.astype(BF16)
    lo = (a - hi.astype(F32)).astype(BF16)
    return hi, lo


def _split3(a):
    hi = a.astype(BF16)
    r = a - hi.astype(F32)
    mid = r.astype(BF16)
    lo = (r - mid.astype(F32)).astype(BF16)
    return hi, mid, lo


def _dot(a, b, dims=NN):
    return lax.dot_general(a, b, dims, preferred_element_type=F32)


def _dot3(a, b_hi, b_lo, dims=NN):
    a_hi, a_lo = _split2(a)
    return _dot(a_hi, b_hi, dims) + (_dot(a_hi, b_lo, dims) + _dot(a_lo, b_hi, dims))


def _wparts(w, passes):
    return (w.astype(BF16),) if passes == 1 else _split2(w)


def _wdot(a, parts, dims=NN):
    if len(parts) == 1:
        return _dot(a.astype(BF16), parts[0], dims)
    return _dot3(a, parts[0], parts[1], dims)


def _dot_exact01(a, ones_bf16):
    hi, mid, lo = _split3(a)
    return _dot(hi, ones_bf16) + (_dot(mid, ones_bf16) + _dot(lo, ones_bf16))


def _mod_row(i, tm, n_prompt, dec_seq):
    np_tiles = n_prompt // tm
    per = dec_seq // tm
    return jnp.where(i < np_tiles, 0, 1 + (i - np_tiles) // per)


def _ada_kernel(c_ref, whi_ref, wlo_ref, b_ref, o_ref):
    c = c_ref[...]
    s = c * jax.nn.sigmoid(c)
    o_ref[0] = _dot3(s, whi_ref[0], wlo_ref[0]) + b_ref[0]


def _adaln_all(cond, w_ada, b_ada):
    depth, d, n6 = w_ada.shape
    rows = cond.shape[0]
    tn = n6 // 4
    whi, wlo = _split2(w_ada)
    return pl.pallas_call(
        _ada_kernel,
        grid=(depth, n6 // tn),
        in_specs=[
            pl.BlockSpec((rows, d), lambda l, j: (0, 0)),
            pl.BlockSpec((1, d, tn), lambda l, j: (l, 0, j)),
            pl.BlockSpec((1, d, tn), lambda l, j: (l, 0, j)),
            pl.BlockSpec((1, 1, tn), lambda l, j: (l, 0, j)),
        ],
        out_specs=pl.BlockSpec((1, rows, tn), lambda l, j: (l, 0, j)),
        out_shape=jax.ShapeDtypeStruct((depth, rows, n6), F32),
        compiler_params=_cparams("parallel", "parallel"),
        name="adaln",
    )(cond, whi, wlo, b_ada.reshape(depth, 1, n6))


def _nmm_kernel(x_ref, g_ref, sc_ref, sh_ref, *rest, n_w):
    w_refs, o_ref, maybe_h = rest[:n_w], rest[n_w], rest[n_w + 1:]
    x = x_ref[...]
    y = x * lax.rsqrt(jnp.mean(x * x, axis=-1, keepdims=True) + NORM_EPS)
    h = y * g_ref[...] * (1.0 + sc_ref[...]) + sh_ref[...]
    o_ref[...] = _wdot(h, tuple(r[...] for r in w_refs))
    if maybe_h:
        maybe_h[0][...] = h.T.astype(BF16)


def _norm_mod_matmul(x, gnorm, scale, shift, w, *, tm, n_prompt, dec_seq, passes, emit_h=False, name):
    n, d = x.shape
    nout = w.shape[1]
    tn = nout if emit_h else nout // 2
    wp = _wparts(w, passes)
    mod_map = lambda j, i: (_mod_row(i, tm, n_prompt, dec_seq), 0, 0)
    out_shape = [jax.ShapeDtypeStruct((n, nout), F32)]
    out_specs = [pl.BlockSpec((tm, tn), lambda j, i: (i, j))]
    if emit_h:
        out_shape.append(jax.ShapeDtypeStruct((d, n), BF16))
        out_specs.append(pl.BlockSpec((d, tm), lambda j, i: (0, i)))
    res = pl.pallas_call(
        functools.partial(_nmm_kernel, n_w=len(wp)),
        grid=(nout // tn, n // tm),
        in_specs=[
            pl.BlockSpec((tm, d), lambda j, i: (i, 0)),
            pl.BlockSpec((1, d), lambda j, i: (0, 0)),
            pl.BlockSpec((None, 1, d), mod_map),
            pl.BlockSpec((None, 1, d), mod_map),
        ] + [pl.BlockSpec((d, tn), lambda j, i: (0, j))] * len(wp),
        out_specs=out_specs,
        out_shape=out_shape,
        compiler_params=_cparams("arbitrary", "arbitrary"),
        name=name,
    )(x, gnorm.reshape(1, d), scale, shift, *wp)
    return res if emit_h else res[0]


def _proj_res_kernel(x_ref, a_ref, gate_ref, w_ref, o_ref):
    o_ref[...] = x_ref[...] + gate_ref[...] * _dot(a_ref[...].astype(BF16), w_ref[...])


def _proj_residual(x, a, gate, w, *, tm, n_prompt, dec_seq, name):
    n, d = x.shape
    k = a.shape[1]
    mod_map = lambda i: (_mod_row(i, tm, n_prompt, dec_seq), 0, 0)
    return pl.pallas_call(
        _proj_res_kernel,
        grid=(n // tm,),
        in_specs=[
            pl.BlockSpec((tm, d), lambda i: (i, 0)),
            pl.BlockSpec((tm, k), lambda i: (i, 0)),
            pl.BlockSpec((None, 1, d), mod_map),
            pl.BlockSpec((k, d), lambda i: (0, 0)),
        ],
        out_specs=pl.BlockSpec((tm, d), lambda i: (i, 0)),
        out_shape=jax.ShapeDtypeStruct((n, d), F32),
        compiler_params=_cparams("parallel"),
        name=name,
    )(x, a, gate, w.astype(BF16))


def _rwkv_prep_kernel(r_ref, k_ref, v_ref, cg_ref, hc_ref, lora_ref,
                      w0_ref, wuph_ref, wupl_ref, a0_ref, auph_ref, aupl_ref, guph_ref, gupl_ref,
                      kk_ref_p, ka_ref_p, rk_ref_p, ones_ref,
                      kk_out, w_out, kd_out, kka_out, bonus_out, g_out, z_out):
    r = r_ref[...]
    k = k_ref[...]
    v = v_ref[...]
    lora = lora_ref[...]
    ones = ones_ref[...]
    wd = lora[:, :A_DECAY_LORA]
    ad = lora[:, A_DECAY_LORA:A_DECAY_LORA + A_ICLR_LORA]
    gd = lora[:, A_DECAY_LORA + A_ICLR_LORA:]

    kk = k * kk_ref_p[...]
    nrm = jnp.sqrt(_dot_exact01(kk * kk, ones))
    kk = kk / jnp.maximum(nrm, 1e-12)
    kk_out[...] = kk

    tw = jnp.tanh(wd)
    bonus = None
    for d in range(2):
        w_pre = w0_ref[d] + _dot3(tw, wuph_ref[d], wupl_ref[d])
        w_out[d] = jnp.exp(-(math.exp(-0.5) * jax.nn.sigmoid(w_pre)))
        a = jax.nn.sigmoid(a0_ref[d] + _dot3(ad, auph_ref[d], aupl_ref[d]))
        kd = k * (1.0 + (a - 1.0) * ka_ref_p[...])
        kd_out[d] = kd
        kka_out[d] = kk * a
        b = _dot_exact01(r * kd * rk_ref_p[...], ones) * v
        bonus = b if bonus is None else bonus + b
    bonus_out[...] = bonus
    g_out[...] = _dot3(jax.nn.sigmoid(gd), guph_ref[...], gupl_ref[...])
    z_out[...] = cg_ref[...] * hc_ref[...]


def _head_ones():
    idx = jnp.arange(D_HALF) // A_HEAD
    return (idx[:, None] == idx[None, :]).astype(BF16)


def _rwkv_prep(y, w0, w_up, a0, a_up, g_up, k_k, k_a, r_k, *, tm):
    n = y.shape[0]
    h = D_HALF
    wuph, wupl = _split2(w_up)
    auph, aupl = _split2(a_up)
    guph, gupl = _split2(g_up)
    col = lambda c: pl.BlockSpec((tm, h), lambda i: (i, c))
    full2 = lambda s: pl.BlockSpec(s, lambda i: (0, 0))
    full3 = lambda s: pl.BlockSpec(s, lambda i: (0, 0, 0))
    tok = pl.BlockSpec((tm, h), lambda i: (i, 0))
    tok2 = pl.BlockSpec((2, tm, h), lambda i: (0, i, 0))
    sds = lambda *s: jax.ShapeDtypeStruct(s, F32)
    return pl.pallas_call(
        _rwkv_prep_kernel,
        grid=(n // tm,),
        in_specs=[col(0), col(1), col(2), col(4), col(5),
                  pl.BlockSpec((tm, LORA_W), lambda i: (i, 6 * h // LORA_W)),
                  full3((2, 1, h)), full3((2, A_DECAY_LORA, h)), full3((2, A_DECAY_LORA, h)),
                  full3((2, 1, h)), full3((2, A_ICLR_LORA, h)), full3((2, A_ICLR_LORA, h)),
                  full2((A_GATE_LORA, h)), full2((A_GATE_LORA, h)),
                  full2((1, h)), full2((1, h)), full2((1, h)), full2((h, h))],
        out_specs=[tok, tok2, tok2, tok2, tok, tok, tok],
        out_shape=[sds(n, h), sds(2, n, h), sds(2, n, h), sds(2, n, h), sds(n, h), sds(n, h), sds(n, h)],
        compiler_params=_cparams("parallel"),
        name="rwkv_prep",
    )(y, y, y, y, y, y,
      w0.reshape(2, 1, h), wuph, wupl, a0.reshape(2, 1, h), auph, aupl, guph, gupl,
      k_k.reshape(1, h), k_a.reshape(1, h), r_k.reshape(1, h), _head_ones())


def _scan_kernel(r_ref, kk_ref, v_ref, w_ref, kd_ref, kka_ref, s0_ref, o_ref, sf_ref, s_scr, *, tc, g):
    d = pl.program_id(0)
    c = pl.program_id(2)
    nk = A_HEAD
    parts = 4

    @pl.when(c == 0)
    def _():
        s_scr[...] = s0_ref[...]

    def row(ref, t, k):
        return jnp.broadcast_to(ref[t, pl.ds(k, 1), :], (g, LANES))

    def tree(xs):
        while len(xs) > 1:
            xs = [xs[i] + xs[i + 1] for i in range(0, len(xs) - 1, 2)] + ([xs[-1]] if len(xs) % 2 else [])
        return xs[0]

    def step(i, carry):
        t = i + d * (tc - 1 - 2 * i)
        acc = [None] * parts
        for k in range(nk):
            p = s_scr[k] * row(kk_ref, t, k)
            acc[k % parts] = p if acc[k % parts] is None else acc[k % parts] + p
        sa = tree(acc)
        vv = v_ref[t]
        acc = [None] * parts
        for k in range(nk):
            sn = s_scr[k] * row(w_ref, t, k) - sa * row(kka_ref, t, k) + vv * row(kd_ref, t, k)
            s_scr[k] = sn
            p = sn * row(r_ref, t, k)
            acc[k % parts] = p if acc[k % parts] is None else acc[k % parts] + p
        o_ref[t] = tree(acc)
        return carry

    lax.fori_loop(0, tc, step, 0)

    @pl.when(c == pl.num_programs(2) - 1)
    def _():
        sf_ref[...] = s_scr[...]


def _wkv_scan(r, kk, v, w, kd, kka, s0, *, tc):
    lg, t, nk, _ = r.shape
    g = v.shape[2]
    nt = t // tc
    tmap = lambda d, l, c: c + d * (nt - 1 - 2 * c)
    shared = lambda rows: pl.BlockSpec((None, tc, rows, LANES), lambda d, l, c: (l, tmap(d, l, c), 0, 0))
    perdir = pl.BlockSpec((None, None, tc, nk, LANES), lambda d, l, c: (d, l, tmap(d, l, c), 0, 0))
    state = pl.BlockSpec((None, None, nk, g, LANES), lambda d, l, c: (d, l, 0, 0, 0))
    return pl.pallas_call(
        functools.partial(_scan_kernel, tc=tc, g=g),
        grid=(2, lg, nt),
        in_specs=[shared(nk), shared(nk), shared(g), perdir, perdir, perdir, state],
        out_specs=[pl.BlockSpec((None, None, tc, g, LANES), lambda d, l, c: (d, l, tmap(d, l, c), 0, 0)), state],
        out_shape=[jax.ShapeDtypeStruct((2, lg, t, g, LANES), F32),
                   jax.ShapeDtypeStruct((2, lg, nk, g, LANES), F32)],
        scratch_shapes=[pltpu.VMEM((nk, g, LANES), F32)],
        compiler_params=_cparams("parallel", "parallel", "arbitrary"),
        name="wkv_scan",
    )(r, kk, v, w, kd, kka, s0)


def _chain_layout(batch):
    chains = batch * A_HEADS
    if chains >= LANES:
        assert chains % LANES == 0
        return chains // LANES, 1
    assert LANES % chains == 0 and A_HEAD % (LANES // chains) == 0
    return 1, LANES // chains


def _to_lanes_k(x, batch, t, lg, vq):
    lead = x.shape[:-2]
    nl = len(lead)
    chains = (batch // lg) * A_HEADS
    x = x.reshape(lead + (lg, 1, batch // lg, t, A_HEADS, A_HEAD))
    x = jnp.broadcast_to(x, lead + (lg, vq, batch // lg, t, A_HEADS, A_HEAD))
    perm = tuple(range(nl)) + tuple(nl + p for p in (0, 3, 5, 1, 2, 4))
    return jnp.transpose(x, perm).reshape(lead + (lg, t, A_HEAD, vq * chains))


def _to_lanes_v(x, batch, t, lg, vq):
    gsz = A_HEAD // vq
    x = x.reshape(lg, batch // lg, t, A_HEADS, vq, gsz)
    x = jnp.transpose(x, (0, 2, 5, 4, 1, 3))
    return x.reshape(lg, t, gsz, vq * (batch // lg) * A_HEADS)


def _from_lanes_v(o, batch, t, lg, vq):
    lead = o.shape[:-4]
    gsz = A_HEAD // vq
    o = o.reshape(lead + (lg, t, gsz, vq, batch // lg, A_HEADS))
    nl = len(lead)
    perm = tuple(range(nl)) + tuple(nl + p for p in (0, 4, 1, 5, 3, 2))
    return jnp.transpose(o, perm).reshape(lead + (batch * t, D_HALF))


def _state_to_lanes(s, lg, vq):
    batch = s.shape[0]
    gsz = A_HEAD // vq
    s = s.reshape(lg, batch // lg, 2, A_HEADS, vq, gsz, A_HEAD)
    s = jnp.transpose(s, (2, 0, 6, 5, 4, 1, 3))
    return s.reshape(2, lg, A_HEAD, gsz, LANES)


def _state_from_lanes(s, batch, lg, vq):
    gsz = A_HEAD // vq
    s = s.reshape(2, lg, A_HEAD, gsz, vq, batch // lg, A_HEADS)
    s = jnp.transpose(s, (1, 5, 0, 6, 4, 3, 2))
    return s.reshape(batch, 2, A_HEADS, A_HEAD, A_HEAD)


def _even_post_kernel(x_ref, o_ref_in, bonus_ref, g_ref, bg_ref, z_ref, zprev_ref, znext_ref,
                      lng_ref, lnb_ref, cw_ref, ones_ref, gate_ref, w_ref, out_ref,
                      *, tm, np_tiles, seq_tiles, dec_tiles):
    i = pl.program_id(0)
    ones = ones_ref[...]
    inv = 1.0 / A_HEAD
    o = o_ref_in[0] + o_ref_in[1]
    mu = _dot_exact01(o, ones) * inv
    xc = o - mu
    var = _dot_exact01(xc * xc, ones) * inv
    on = xc * lax.rsqrt(var + GN_EPS) * lng_ref[...] + lnb_ref[...]
    ya = (on + bonus_ref[...]) * g_ref[...]

    pos = jnp.where(i < np_tiles, i % seq_tiles, (i - np_tiles) % dec_tiles)
    last = jnp.where(i < np_tiles, seq_tiles - 1, dec_tiles - 1)
    z = z_ref[...]
    rows = lax.broadcasted_iota(jnp.int32, z.shape, 0)
    prev_row = jnp.where(pos == 0, 0.0, zprev_ref[pl.ds(SUBLANES - 1, 1), :])
    next_row = jnp.where(pos == last, 0.0, znext_ref[pl.ds(0, 1), :])
    zp = jnp.where(rows == 0, prev_row, pltpu.roll(z, 1, axis=0))
    zn = jnp.where(rows == tm - 1, next_row, pltpu.roll(z, tm - 1, axis=0))
    yb = bg_ref[...] * (zp * cw_ref[pl.ds(0, 1), :] + z * cw_ref[pl.ds(1, 1), :] + zn * cw_ref[pl.ds(2, 1), :])

    h = D_HALF
    y = _dot(ya.astype(BF16), w_ref[pl.ds(0, h), :]) + _dot(yb.astype(BF16), w_ref[pl.ds(h, h), :])
    out_ref[...] = x_ref[...] + gate_ref[...] * y


def _even_post(x, o2, bonus, g, y, z, ln_g, ln_b, conv_w, gate, w_out, *, tm, n_prompt, seq, dec_seq):
    n, d = x.shape
    h = D_HALF
    nblk8 = n // SUBLANES
    per8 = tm // SUBLANES
    tok = pl.BlockSpec((tm, h), lambda i: (i, 0))
    full2 = lambda s: pl.BlockSpec(s, lambda i: (0, 0))
    mod_map = lambda i: (_mod_row(i, tm, n_prompt, dec_seq), 0, 0)
    return pl.pallas_call(
        functools.partial(_even_post_kernel, tm=tm, np_tiles=n_prompt // tm, seq_tiles=seq // tm,
                          dec_tiles=dec_seq // tm),
        grid=(n // tm,),
        in_specs=[
            pl.BlockSpec((tm, d), lambda i: (i, 0)),
            pl.BlockSpec((2, tm, h), lambda i: (0, i, 0)),
            tok, tok,
            pl.BlockSpec((tm, h), lambda i: (i, 3)),
            tok,
            pl.BlockSpec((SUBLANES, h), lambda i: (jnp.maximum(i * per8 - 1, 0), 0)),
            pl.BlockSpec((SUBLANES, h), lambda i: (jnp.minimum((i + 1) * per8, nblk8 - 1), 0)),
            full2((1, h)), full2((1, h)), full2((SUBLANES, h)), full2((h, h)),
            pl.BlockSpec((None, 1, d), mod_map),
            full2((d, d)),
        ],
        out_specs=pl.BlockSpec((tm, d), lambda i: (i, 0)),
        out_shape=jax.ShapeDtypeStruct((n, d), F32),
        compiler_params=_cparams("parallel"),
        name="even_post",
    )(x, o2, bonus, g, y, z, z, z, ln_g.reshape(1, h), ln_b.reshape(1, h),
      jnp.pad(conv_w, ((0, SUBLANES - conv_w.shape[0]), (0, 0))), _head_ones(), gate, w_out.astype(BF16))


def _attn_kernel(lam_ref, q_ref, k_ref, v_ref, subln_ref, o_ref, *, out_scale):
    lam = lam_ref[0]
    hd2 = 2 * C_HD
    for h in range(C_HEADS):
        parts = []
        for j in range(2):
            lo = h * hd2 + j * C_HD
            s = _dot(q_ref[:, lo:lo + C_HD], k_ref[:, lo:lo + C_HD], NT)
            e = jnp.exp(s - jnp.max(s, axis=-1, keepdims=True))
            inv = 1.0 / jnp.sum(e, axis=-1, keepdims=True)
            parts.append(_dot(e.astype(BF16), v_ref[:, h * hd2:(h + 1) * hd2]) * inv)
        oh = parts[0] - lam * parts[1]
        y = oh * lax.rsqrt(jnp.mean(oh * oh, axis=-1, keepdims=True) + NORM_EPS)
        o_ref[:, h * hd2:(h + 1) * hd2] = y * subln_ref[...] * out_scale


def _diff_attention(lam, q, k, v, subln, lam_init, *, batch, tq_total, tq):
    d = q.shape[1]
    tk = k.shape[1]
    nq = tq_total // tq
    return pl.pallas_call(
        functools.partial(_attn_kernel, out_scale=1.0 - lam_init),
        grid=(batch, nq),
        in_specs=[
            pl.BlockSpec(memory_space=pltpu.SMEM),
            pl.BlockSpec((tq, d), lambda b, i: (b * nq + i, 0)),
            pl.BlockSpec((None, tk, d), lambda b, i: (b, 0, 0)),
            pl.BlockSpec((None, tk, d), lambda b, i: (b, 0, 0)),
            pl.BlockSpec((1, 2 * C_HD), lambda b, i: (0, 0)),
        ],
        out_specs=pl.BlockSpec((tq, d), lambda b, i: (b * nq + i, 0)),
        out_shape=jax.ShapeDtypeStruct((batch * tq_total, d), F32),
        compiler_params=_cparams("parallel", "parallel"),
        name="diff_attn",
    )(lam.reshape(1), q, k, v, subln.reshape(1, 2 * C_HD))


def _axial_rope_tables(n_tok):
    rows = n_tok // GRID_W
    row = jnp.repeat(jnp.arange(rows, dtype=F32), GRID_W)
    col = jnp.tile(jnp.arange(GRID_W, dtype=F32), rows)
    nf = C_HD // 4
    inv = ROPE_BASE ** (-jnp.arange(nf, dtype=F32) / nf)
    ar = row[:, None] * inv[None, :]
    ac = col[:, None] * inv[None, :]
    ang = jnp.concatenate([ar, ar, ac, ac], axis=-1)
    return jnp.cos(ang), jnp.sin(ang)


def _apply_rope(x, cos, sin):
    h = x.shape[-1] // 2
    q = h // 2
    xr = jnp.concatenate([-x[..., q:h], x[..., :q], -x[..., h + q:], x[..., h:h + q]], axis=-1)
    return x * cos[None, :, None, None, :] + xr * sin[None, :, None, None, :]


N_TOP = P_TOPK + 1
CAND_PAIRS = [(i, j) for i in range(N_TOP) for j in range(N_TOP) if (i + 1) * (j + 1) <= N_TOP]
CAND_ROWS = -(-len(CAND_PAIRS) // SUBLANES) * SUBLANES
TOP_ROWS = -(-N_TOP // SUBLANES) * SUBLANES


def _take_top(x, n, emit, want_rank=False):
    rank = jnp.full(x.shape, float(n), F32) if want_rank else None
    for it in range(n):
        m = jnp.max(x, axis=0, keepdims=True)
        emit(it, m)
        hit = x == m
        if want_rank:
            rank = jnp.where(hit, float(it), rank)
        if it + 1 < n:
            x = jnp.where(hit, NEG_INF, x)
    return rank


def _pack_bf16_rows(x):
    return pltpu.bitcast(x.astype(BF16), jnp.int32)


def _peer_select_kernel(q_ref, skh_ref, skl_ref, l_out, c1_out, r2_out, e2_out, top_scr, cand_scr):
    half = P_QDIM // 2
    for h in range(P_HEADS):
        sT = []
        for j in range(2):
            hj = 2 * h + j
            qs = q_ref[:, hj * half:(hj + 1) * half]
            q_hi, q_lo = _split2(qs)
            s = _dot(skh_ref[hj], q_hi, NT) + (_dot(skh_ref[hj], q_lo, NT) + _dot(skl_ref[hj], q_hi, NT))
            sT.append(s)

            def emit(it, m, j=j):
                top_scr[j, pl.ds(it, 1), :] = m
            rank = _take_top(s, N_TOP, emit, want_rank=(j == 1))

        cand_scr[...] = jnp.full(cand_scr.shape, NEG_INF, F32)
        for c, (i, j) in enumerate(CAND_PAIRS):
            cand_scr[pl.ds(c, 1), :] = top_scr[0, pl.ds(i, 1), :] + top_scr[1, pl.ds(j, 1), :]
        a1 = top_scr[0, pl.ds(0, 1), :]
        b1 = top_scr[1, pl.ds(0, 1), :]
        mx = a1 + b1
        st = {"z": None, "c16": None, "c17": None}

        def emit_c(it, m):
            if it < P_TOPK:
                e = jnp.exp(m - mx)
                st["z"] = e if st["z"] is None else st["z"] + e
            if it == P_TOPK - 1:
                st["c16"] = m
            if it == P_TOPK:
                st["c17"] = m
        _take_top(cand_scr[...], N_TOP, emit_c)
        tau = 0.5 * (st["c16"] + st["c17"])
        thr = tau - sT[0]
        cnt = jnp.zeros(thr.shape, F32)
        for k in range(N_TOP):
            cnt = cnt + jnp.where(top_scr[1, pl.ds(k, 1), :] >= thr, 1.0, 0.0)
        l_out[h] = cnt
        c1_out[h] = jnp.exp(sT[0] - a1) / st["z"]
        e2 = jnp.exp(sT[1] - b1)
        for g in range(e2.shape[1] // LANES):
            r2_out[h, g] = _pack_bf16_rows(rank[:, g * LANES:(g + 1) * LANES])
            e2_out[h, g] = _pack_bf16_rows(e2[:, g * LANES:(g + 1) * LANES])


def _peer_select(q, sub_keys, *, tm):
    n = q.shape[0]
    sk = sub_keys.reshape(2 * P_HEADS, P_NKEYS, P_QDIM // 2)
    skh, skl = _split2(sk)
    out = pl.BlockSpec((P_HEADS, P_NKEYS, tm), lambda i: (0, 0, i))
    sds = jax.ShapeDtypeStruct((P_HEADS, P_NKEYS, n), F32)
    out_g = pl.BlockSpec((P_HEADS, tm // LANES, P_NKEYS // 2, LANES), lambda i: (0, i, 0, 0))
    sds_g = jax.ShapeDtypeStruct((P_HEADS, n // LANES, P_NKEYS // 2, LANES), jnp.int32)
    return pl.pallas_call(
        _peer_select_kernel,
        grid=(n // tm,),
        in_specs=[pl.BlockSpec((tm, P_HEADS * P_QDIM), lambda i: (i, 0)),
                  pl.BlockSpec(sk.shape, lambda i: (0, 0, 0)),
                  pl.BlockSpec(sk.shape, lambda i: (0, 0, 0))],
        out_specs=[out, out, out_g, out_g],
        out_shape=[sds, sds, sds_g, sds_g],
        scratch_shapes=[pltpu.VMEM((2, TOP_ROWS, tm), F32), pltpu.VMEM((CAND_ROWS, tm), F32)],
        compiler_params=_cparams("parallel"),
        name="peer_select",
    )(q, skh, skl)


PAIR = 2
PAIR_ROWS = PAIR * P_NKEYS
I2_HALF = P_NKEYS // 2


def _gelu(x):
    return 0.5 * x * (1.0 + lax.erf(x * (1.0 / math.sqrt(2.0))))


def _peer_dense_kernel(hbt_ref, x_ref, gate_ref, ufirst_ref, unext_ref, vtprev_ref, vtlast_ref,
                       l_ref, c1_ref, r2_ref, e2_ref, o_ref, acc_ref, a_even, a_odd, act_even, act_odd,
                       *, stages, tm):
    c = pl.program_id(1)

    def put_act(act_ref, act):
        for q in range(stages):
            for lg in range(tm // LANES):
                act_ref[q, lg] = act[q * PAIR_ROWS:(q + 1) * PAIR_ROWS, lg * LANES:(lg + 1) * LANES]

    @pl.when(c == 0)
    def _():
        acc_ref[...] = jnp.zeros(acc_ref.shape, F32)
        a_odd[...] = jnp.zeros(a_odd.shape, BF16)
        put_act(act_even, _gelu(_dot(ufirst_ref[...], hbt_ref[...])))

    def chunk(act_cur, act_next, a_cur, a_prev):
        put_act(act_next, _gelu(_dot(unext_ref[...], hbt_ref[...])))
        for p in range(stages):
            for lg in range(tm // LANES):
                ls = slice(lg * LANES, (lg + 1) * LANES)
                for half in range(P_NKEYS // I2_HALF):
                    words = slice(half * I2_HALF // 2, (half + 1) * I2_HALF // 2)
                    w = [None] * PAIR
                    for h in range(P_HEADS):
                        r2 = pltpu.bitcast(r2_ref[h, lg, words, :], BF16)
                        e2 = pltpu.bitcast(e2_ref[h, lg, words, :], BF16)
                        for j in range(PAIR):
                            row = pl.ds(PAIR * p + j, 1)
                            cnt = jnp.broadcast_to(l_ref[h, row, ls], (I2_HALF, LANES)).astype(BF16)
                            c1 = jnp.broadcast_to(c1_ref[h, row, ls], (I2_HALF, LANES)).astype(BF16)
                            term = jnp.where(r2 < cnt, e2 * c1, jnp.zeros_like(e2))
                            w[j] = term if w[j] is None else w[j] + term
                    for j in range(PAIR):
                        off = j * P_NKEYS + half * I2_HALF
                        rs = slice(p * PAIR_ROWS + off, p * PAIR_ROWS + off + I2_HALF)
                        a_cur[rs, ls] = w[j] * act_cur[p, lg, off:off + I2_HALF, :].astype(BF16)
        acc_ref[...] += _dot(vtprev_ref[...], a_prev[...])

    @pl.when(c % 2 == 0)
    def _():
        chunk(act_even, act_odd, a_even, a_odd)

    @pl.when(c % 2 == 1)
    def _():
        chunk(act_odd, act_even, a_odd, a_even)

    @pl.when(c == pl.num_programs(1) - 1)
    def _():
        total = acc_ref[...] + _dot(vtlast_ref[...], a_odd[...])
        o_ref[...] = x_ref[...] + gate_ref[...] * total.T


def _peer_dense(x, hbt, gate, u, v, cnt, c1, r2, e2, *, tm, stages, n_prompt, dec_seq):
    n, d = x.shape
    ec = stages * PAIR_ROWS
    n_chunks = P_EXPERTS // ec
    assert n_chunks % 2 == 0
    u_bf = u.astype(BF16)
    vt_bf = v.astype(BF16).T
    mod_map = lambda i, c: (_mod_row(i, tm, n_prompt, dec_seq), 0, 0)
    small = pl.BlockSpec((P_HEADS, stages * PAIR, tm), lambda i, c: (0, c, i))
    big = pl.BlockSpec((P_HEADS, tm // LANES, P_NKEYS // 2, LANES), lambda i, c: (0, i, 0, 0))
    act_shape = (stages, tm // LANES, PAIR_ROWS, LANES)
    return pl.pallas_call(
        functools.partial(_peer_dense_kernel, stages=stages, tm=tm),
        grid=(n // tm, n_chunks),
        in_specs=[
            pl.BlockSpec((d, tm), lambda i, c: (0, i)),
            pl.BlockSpec((tm, d), lambda i, c: (i, 0)),
            pl.BlockSpec((None, 1, d), mod_map),
            pl.BlockSpec((ec, d), lambda i, c: (0, 0)),
            pl.BlockSpec((ec, d), lambda i, c: (jnp.minimum(c + 1, n_chunks - 1), 0)),
            pl.BlockSpec((d, ec), lambda i, c: (0, jnp.maximum(c - 1, 0))),
            pl.BlockSpec((d, ec), lambda i, c: (0, n_chunks - 1)),
            small, small, big, big,
        ],
        out_specs=pl.BlockSpec((tm, d), lambda i, c: (i, 0)),
        out_shape=jax.ShapeDtypeStruct((n, d), F32),
        scratch_shapes=[pltpu.VMEM((d, tm), F32), pltpu.VMEM((ec, tm), BF16), pltpu.VMEM((ec, tm), BF16),
                        pltpu.VMEM(act_shape, F32), pltpu.VMEM(act_shape, F32)],
        compiler_params=_cparams("parallel", "arbitrary"),
        name="peer_dense",
    )(hbt, x, gate, u_bf, u_bf, vt_bf, vt_bf, cnt, c1, r2, e2)


def _final_norm_kernel(x_ref, g_ref, o_ref):
    x = x_ref[...]
    o_ref[...] = x * lax.rsqrt(jnp.mean(x * x, axis=-1, keepdims=True) + NORM_EPS) * g_ref[...]


def _final_norm(x, g, *, tm):
    n, d = x.shape
    return pl.pallas_call(
        _final_norm_kernel,
        grid=(n // tm,),
        in_specs=[pl.BlockSpec((tm, d), lambda i: (i, 0)), pl.BlockSpec((1, d), lambda i: (0, 0))],
        out_specs=pl.BlockSpec((tm, d), lambda i: (i, 0)),
        out_shape=jax.ShapeDtypeStruct((n, d), F32),
        compiler_params=_cparams("parallel"),
        name="final_norm",
    )(x, g.reshape(1, d))


TOKEN_TILE = 256
DENSE_TILE = 512


def _tiles(n_prompt, seq, dec_seq):
    tm = math.gcd(math.gcd(seq, dec_seq), TOKEN_TILE)
    tm_dense = math.gcd(math.gcd(n_prompt, dec_seq), DENSE_TILE)
    return tm, tm_dense


def kernel(x_prompt, x_sample, c, state_rwkv, cache_k, cache_v, c_ctx, w_ada, b_ada, norm_mix, norm_ffn, w_in_even, rwkv_w0, rwkv_w_up, rwkv_a0, rwkv_a_up, rwkv_g_up, rwkv_k_k, rwkv_k_a, rwkv_r_k, rwkv_ln_g, rwkv_ln_b, conv_w, w_out_even, w_in_odd, diff_lam, diff_subln, w_out_odd, peer_w_q, peer_sub_keys, peer_u, peer_v, final_norm):
    bp, seq, d = x_prompt.shape
    bs, dec_seq, _ = x_sample.shape
    depth = w_ada.shape[0]
    n_prompt = bp * seq
    n_sample = bs * dec_seq
    tm, tm_dense = _tiles(n_prompt, seq, dec_seq)
    tok = dict(n_prompt=n_prompt, dec_seq=dec_seq)

    x = jnp.concatenate([x_prompt.reshape(n_prompt, d), x_sample.reshape(n_sample, d)], axis=0)

    rows = 1 + bs
    rows_pad = -(-rows // SUBLANES) * SUBLANES
    cond = jnp.zeros((rows_pad, d), F32).at[0].set(c_ctx).at[1:rows].set(c)
    mod = _adaln_all(cond, w_ada, b_ada)[:, :rows].reshape(depth, rows, 6, 1, d)
    mods = [[mod[l, :, m] for m in range(6)] for l in range(depth)]

    h = D_HALF
    perm = jnp.concatenate([jnp.arange(0, 3 * h), jnp.arange(3 * h + LORA_W, 6 * h + LORA_W),
                            jnp.arange(3 * h, 3 * h + LORA_W)])

    lg_p, vq_p = _chain_layout(bp)
    lg_s, vq_s = _chain_layout(bs)
    cos, sin = _axial_rope_tables(dec_seq)
    scale = C_HD ** -0.5

    new_s, new_k, new_v = [], [], []
    for l in range(depth):
        i = l // 2
        sh1, sc1, g1, sh2, sc2, g2 = mods[l]
        if l % 2 == 0:
            y = _norm_mod_matmul(x, norm_mix[l], sc1, sh1, w_in_even[i][:, perm], tm=tm, passes=1, name="in_even", **tok)
            kk, w, kd, kka, bonus, g, z = _rwkv_prep(
                y, rwkv_w0[i], rwkv_w_up[i], rwkv_a0[i], rwkv_a_up[i], rwkv_g_up[i],
                rwkv_k_k[i], rwkv_k_a[i], rwkv_r_k[i].reshape(-1), tm=tm)
            r = y[:, 0:h]
            v = y[:, 2 * h:3 * h]
            outs = []
            for (lo, n_tok, batch, t, lg, vq, s0) in (
                    (0, n_prompt, bp, seq, lg_p, vq_p, None),
                    (n_prompt, n_sample, bs, dec_seq, lg_s, vq_s, state_rwkv[:, i])):
                sl = slice(lo, lo + n_tok)
                lk = lambda a: _to_lanes_k(a, batch, t, lg, vq)
                gsz = A_HEAD // vq
                s0l = (jnp.zeros((2, lg, A_HEAD, gsz, LANES), F32) if s0 is None
                       else _state_to_lanes(s0, lg, vq))
                o, sf = _wkv_scan(lk(r[sl]), lk(kk[sl]), _to_lanes_v(v[sl], batch, t, lg, vq),
                                  lk(w[:, sl]), lk(kd[:, sl]), lk(kka[:, sl]), s0l, tc=min(16, t))
                outs.append(_from_lanes_v(o, batch, t, lg, vq))
                if s0 is None:
                    new_s.append(_state_from_lanes(sf, batch, lg, vq))
            o2 = jnp.concatenate(outs, axis=1)
            x = _even_post(x, o2, bonus, g, y, z, rwkv_ln_g[i], rwkv_ln_b[i], conv_w[i], g1, w_out_even[i],
                           tm=tm, n_prompt=n_prompt, seq=seq, dec_seq=dec_seq)
        else:
            lam_init = 0.8 - 0.6 * math.exp(-0.3 * l)
            lp = diff_lam[i].astype(F32)
            lam = jnp.exp(jnp.sum(lp[0] * lp[1])) - jnp.exp(jnp.sum(lp[2] * lp[3])) + lam_init
            y = _norm_mod_matmul(x, norm_mix[l], sc1, sh1, w_in_odd[i], tm=tm, passes=1, name="in_odd", **tok)
            q, k, v = y[:, 0:d], y[:, d:2 * d], y[:, 2 * d:3 * d]
            new_k.append(k[:n_prompt].reshape(bp, seq, C_HEADS, 2, C_HD))
            new_v.append(v[:n_prompt].reshape(bp, seq, C_HEADS, 2 * C_HD))
            o_p = _diff_attention(lam, (q[:n_prompt] * scale).astype(BF16),
                                  k[:n_prompt].reshape(bp, seq, d).astype(BF16),
                                  v[:n_prompt].reshape(bp, seq, d).astype(BF16),
                                  diff_subln[i], lam_init, batch=bp, tq_total=seq, tq=min(seq, 256))
            shp = (bs, dec_seq, C_HEADS, 2, C_HD)
            qs = _apply_rope(q[n_prompt:].reshape(shp), cos, sin).reshape(n_sample, d)
            ks = _apply_rope(k[n_prompt:].reshape(shp), cos, sin).reshape(bs, dec_seq, d)
            k_all = jnp.concatenate([ks, cache_k[:, i].reshape(bs, -1, d)], axis=1).astype(BF16)
            v_all = jnp.concatenate([v[n_prompt:].reshape(bs, dec_seq, d), cache_v[:, i].reshape(bs, -1, d)],
                                    axis=1).astype(BF16)
            o_s = _diff_attention(lam, (qs * scale).astype(BF16), k_all, v_all, diff_subln[i], lam_init,
                                  batch=bs, tq_total=dec_seq, tq=min(dec_seq, 256))
            x = _proj_residual(x, jnp.concatenate([o_p, o_s], axis=0), g1, w_out_odd[i], tm=tm,
                               name="out_odd", **tok)

        qp, hbt = _norm_mod_matmul(x, norm_ffn[l], sc2, sh2, peer_w_q[l], tm=tm, passes=1, emit_h=True,
                                  name="peer_q", **tok)
        cnt_, c1_, r2_, e2_ = _peer_select(qp, peer_sub_keys[l], tm=tm)
        x = _peer_dense(x, hbt, g2, peer_u[l], peer_v[l], cnt_, c1_, r2_, e2_, tm=tm_dense, stages=4, **tok)

    yfin = _final_norm(x, final_norm, tm=tm)
    y_prompt = yfin[:n_prompt].reshape(bp, seq, d)
    y_sample = yfin[n_prompt:].reshape(bs, dec_seq, d)
    return (y_prompt, y_sample, jnp.stack(new_s, axis=1), jnp.stack(new_k, axis=1), jnp.stack(new_v, axis=1))
```

```python
import functools
import math

import jax
import jax.numpy as jnp
from jax import lax
from jax.experimental import pallas as pl
from jax.experimental.pallas import tpu as pltpu

D_MODEL = 1024
GRID_W = 64
ROPE_BASE = 10000.0
NORM_EPS = 1e-6
D_HALF = D_MODEL // 2
A_HEAD = 64
A_HEADS = D_HALF // A_HEAD
A_DECAY_LORA = 64
A_ICLR_LORA = 64
A_GATE_LORA = 128
GN_EPS = 64e-5
C_HEADS = 8
C_HD = D_MODEL // C_HEADS // 2
P_HEADS = 8
P_NKEYS = 128
P_EXPERTS = P_NKEYS * P_NKEYS
P_QDIM = 256
P_TOPK = 16
LORA_W = A_DECAY_LORA + A_ICLR_LORA + A_GATE_LORA
EVEN_IN = 6 * D_HALF + LORA_W

LANES = 128
SUBLANES = 8
VMEM_LIMIT = 56 * 1024 * 1024

F32 = jnp.float32
BF16 = jnp.bfloat16
NN = (((1,), (0,)), ((), ()))
NT = (((1,), (1,)), ((), ()))
NEG_INF = float("-inf")


def _cparams(*sem):
    return pltpu.CompilerParams(dimension_semantics=sem, vmem_limit_bytes=VMEM_LIMIT)


def _split2(a):
    hi = a.astype(BF16)
    lo = (a - hi.astype(F32)).astype(BF16)
    return hi, lo


def _split3(a):
    hi = a.astype(BF16)
    r = a - hi.astype(F32)
    mid = r.astype(BF16)
    lo = (r - mid.astype(F32)).astype(BF16)
    return hi, mid, lo


def _dot(a, b, dims=NN):
    return lax.dot_general(a, b, dims, preferred_element_type=F32)


def _dot3(a, b_hi, b_lo, dims=NN):
    a_hi, a_lo = _split2(a)
    return _dot(a_hi, b_hi, dims) + (_dot(a_hi, b_lo, dims) + _dot(a_lo, b_hi, dims))


def _wparts(w, passes):
    return (w.astype(BF16),) if passes == 1 else _split2(w)


def _wdot(a, parts, dims=NN):
    if len(parts) == 1:
        return _dot(a.astype(BF16), parts[0], dims)
    return _dot3(a, parts[0], parts[1], dims)


def _dot_exact01(a, ones_bf16):
    hi, mid, lo = _split3(a)
    return _dot(hi, ones_bf16) + (_dot(mid, ones_bf16) + _dot(lo, ones_bf16))


def _mod_row(i, tm, n_prompt, dec_seq):
    np_tiles = n_prompt // tm
    per = dec_seq // tm
    return jnp.where(i < np_tiles, 0, 1 + (i - np_tiles) // per)


def _ada_kernel(c_ref, whi_ref, wlo_ref, b_ref, o_ref):
    c = c_ref[...]
    s = c * jax.nn.sigmoid(c)
    o_ref[0] = _dot3(s, whi_ref[0], wlo_ref[0]) + b_ref[0]


def _adaln_all(cond, w_ada, b_ada):
    depth, d, n6 = w_ada.shape
    rows = cond.shape[0]
    tn = n6 // 4
    whi, wlo = _split2(w_ada)
    return pl.pallas_call(
        _ada_kernel,
        grid=(depth, n6 // tn),
        in_specs=[
            pl.BlockSpec((rows, d), lambda l, j: (0, 0)),
            pl.BlockSpec((1, d, tn), lambda l, j: (l, 0, j)),
            pl.BlockSpec((1, d, tn), lambda l, j: (l, 0, j)),
            pl.BlockSpec((1, 1, tn), lambda l, j: (l, 0, j)),
        ],
        out_specs=pl.BlockSpec((1, rows, tn), lambda l, j: (l, 0, j)),
        out_shape=jax.ShapeDtypeStruct((depth, rows, n6), F32),
        compiler_params=_cparams("parallel", "parallel"),
        name="adaln",
    )(cond, whi, wlo, b_ada.reshape(depth, 1, n6))


def _nmm_kernel(x_ref, g_ref, sc_ref, sh_ref, *rest, n_w):
    w_refs, o_ref, maybe_h = rest[:n_w], rest[n_w], rest[n_w + 1:]
    x = x_ref[...]
    y = x * lax.rsqrt(jnp.mean(x * x, axis=-1, keepdims=True) + NORM_EPS)
    h = y * g_ref[...] * (1.0 + sc_ref[...]) + sh_ref[...]
    o_ref[...] = _wdot(h, tuple(r[...] for r in w_refs))
    if maybe_h:
        maybe_h[0][...] = h.T.astype(BF16)


def _norm_mod_matmul(x, gnorm, scale, shift, w, *, tm, n_prompt, dec_seq, passes, emit_h=False, name):
    n, d = x.shape
    nout = w.shape[1]
    tn = nout if emit_h else nout // 2
    wp = _wparts(w, passes)
    mod_map = lambda j, i: (_mod_row(i, tm, n_prompt, dec_seq), 0, 0)
    out_shape = [jax.ShapeDtypeStruct((n, nout), F32)]
    out_specs = [pl.BlockSpec((tm, tn), lambda j, i: (i, j))]
    if emit_h:
        out_shape.append(jax.ShapeDtypeStruct((d, n), BF16))
        out_specs.append(pl.BlockSpec((d, tm), lambda j, i: (0, i)))
    res = pl.pallas_call(
        functools.partial(_nmm_kernel, n_w=len(wp)),
        grid=(nout // tn, n // tm),
        in_specs=[
            pl.BlockSpec((tm, d), lambda j, i: (i, 0)),
            pl.BlockSpec((1, d), lambda j, i: (0, 0)),
            pl.BlockSpec((None, 1, d), mod_map),
            pl.BlockSpec((None, 1, d), mod_map),
        ] + [pl.BlockSpec((d, tn), lambda j, i: (0, j))] * len(wp),
        out_specs=out_specs,
        out_shape=out_shape,
        compiler_params=_cparams("arbitrary", "arbitrary"),
        name=name,
    )(x, gnorm.reshape(1, d), scale, shift, *wp)
    return res if emit_h else res[0]


def _proj_res_kernel(x_ref, a_ref, gate_ref, w_ref, o_ref):
    o_ref[...] = x_ref[...] + gate_ref[...] * _dot(a_ref[...].astype(BF16), w_ref[...])


def _proj_residual(x, a, gate, w, *, tm, n_prompt, dec_seq, name):
    n, d = x.shape
    k = a.shape[1]
    mod_map = lambda i: (_mod_row(i, tm, n_prompt, dec_seq), 0, 0)
    return pl.pallas_call(
        _proj_res_kernel,
        grid=(n // tm,),
        in_specs=[
            pl.BlockSpec((tm, d), lambda i: (i, 0)),
            pl.BlockSpec((tm, k), lambda i: (i, 0)),
            pl.BlockSpec((None, 1, d), mod_map),
            pl.BlockSpec((k, d), lambda i: (0, 0)),
        ],
        out_specs=pl.BlockSpec((tm, d), lambda i: (i, 0)),
        out_shape=jax.ShapeDtypeStruct((n, d), F32),
        compiler_params=_cparams("parallel"),
        name=name,
    )(x, a, gate, w.astype(BF16))


def _rwkv_prep_kernel(r_ref, k_ref, v_ref, cg_ref, hc_ref, lora_ref,
                      w0_ref, wuph_ref, wupl_ref, a0_ref, auph_ref, aupl_ref, guph_ref, gupl_ref,
                      kk_ref_p, ka_ref_p, rk_ref_p, ones_ref,
                      kk_out, w_out, kd_out, kka_out, bonus_out, g_out, z_out):
    r = r_ref[...]
    k = k_ref[...]
    v = v_ref[...]
    lora = lora_ref[...]
    ones = ones_ref[...]
    wd = lora[:, :A_DECAY_LORA]
    ad = lora[:, A_DECAY_LORA:A_DECAY_LORA + A_ICLR_LORA]
    gd = lora[:, A_DECAY_LORA + A_ICLR_LORA:]

    kk = k * kk_ref_p[...]
    nrm = jnp.sqrt(_dot_exact01(kk * kk, ones))
    kk = kk / jnp.maximum(nrm, 1e-12)
    kk_out[...] = kk

    tw = jnp.tanh(wd)
    bonus = None
    for d in range(2):
        w_pre = w0_ref[d] + _dot3(tw, wuph_ref[d], wupl_ref[d])
        w_out[d] = jnp.exp(-(math.exp(-0.5) * jax.nn.sigmoid(w_pre)))
        a = jax.nn.sigmoid(a0_ref[d] + _dot3(ad, auph_ref[d], aupl_ref[d]))
        kd = k * (1.0 + (a - 1.0) * ka_ref_p[...])
        kd_out[d] = kd
        kka_out[d] = kk * a
        b = _dot_exact01(r * kd * rk_ref_p[...], ones) * v
        bonus = b if bonus is None else bonus + b
    bonus_out[...] = bonus
    g_out[...] = _dot3(jax.nn.sigmoid(gd), guph_ref[...], gupl_ref[...])
    z_out[...] = cg_ref[...] * hc_ref[...]


def _head_ones():
    idx = jnp.arange(D_HALF) // A_HEAD
    return (idx[:, None] == idx[None, :]).astype(BF16)


def _rwkv_prep(y, w0, w_up, a0, a_up, g_up, k_k, k_a, r_k, *, tm):
    n = y.shape[0]
    h = D_HALF
    wuph, wupl = _split2(w_up)
    auph, aupl = _split2(a_up)
    guph, gupl = _split2(g_up)
    col = lambda c: pl.BlockSpec((tm, h), lambda i: (i, c))
    full2 = lambda s: pl.BlockSpec(s, lambda i: (0, 0))
    full3 = lambda s: pl.BlockSpec(s, lambda i: (0, 0, 0))
    tok = pl.BlockSpec((tm, h), lambda i: (i, 0))
    tok2 = pl.BlockSpec((2, tm, h), lambda i: (0, i, 0))
    sds = lambda *s: jax.ShapeDtypeStruct(s, F32)
    return pl.pallas_call(
        _rwkv_prep_kernel,
        grid=(n // tm,),
        in_specs=[col(0), col(1), col(2), col(4), col(5),
                  pl.BlockSpec((tm, LORA_W), lambda i: (i, 6 * h // LORA_W)),
                  full3((2, 1, h)), full3((2, A_DECAY_LORA, h)), full3((2, A_DECAY_LORA, h)),
                  full3((2, 1, h)), full3((2, A_ICLR_LORA, h)), full3((2, A_ICLR_LORA, h)),
                  full2((A_GATE_LORA, h)), full2((A_GATE_LORA, h)),
                  full2((1, h)), full2((1, h)), full2((1, h)), full2((h, h))],
        out_specs=[tok, tok2, tok2, tok2, tok, tok, tok],
        out_shape=[sds(n, h), sds(2, n, h), sds(2, n, h), sds(2, n, h), sds(n, h), sds(n, h), sds(n, h)],
        compiler_params=_cparams("parallel"),
        name="rwkv_prep",
    )(y, y, y, y, y, y,
      w0.reshape(2, 1, h), wuph, wupl, a0.reshape(2, 1, h), auph, aupl, guph, gupl,
      k_k.reshape(1, h), k_a.reshape(1, h), r_k.reshape(1, h), _head_ones())


SCAN_ROW_BLOCK = 32


def _scan_kernel(r_ref, kk_ref, v_ref, w_ref, kd_ref, kka_ref, s0_ref, o_ref, sf_ref, s_scr, *, tc, g):
    d = pl.program_id(0)
    c = pl.program_id(2)
    nk = A_HEAD
    parts = 4

    @pl.when(c == 0)
    def _():
        s_scr[...] = s0_ref[...]

    gb = min(g, SCAN_ROW_BLOCK)

    def row(ref, t, k):
        return jnp.broadcast_to(ref[t, pl.ds(k, 1), :], (gb, LANES))

    def tree(xs):
        while len(xs) > 1:
            xs = [xs[i] + xs[i + 1] for i in range(0, len(xs) - 1, 2)] + ([xs[-1]] if len(xs) % 2 else [])
        return xs[0]

    def step(i, carry):
        t = i + d * (tc - 1 - 2 * i)
        for g0 in range(0, g, gb):
            rows = slice(g0, g0 + gb)
            acc = [None] * parts
            for k in range(nk):
                p = s_scr[k, rows, :] * row(kk_ref, t, k)
                acc[k % parts] = p if acc[k % parts] is None else acc[k % parts] + p
            sa = tree(acc)
            vv = v_ref[t, rows, :]
            acc = [None] * parts
            for k in range(nk):
                sn = s_scr[k, rows, :] * row(w_ref, t, k) - sa * row(kka_ref, t, k) + vv * row(kd_ref, t, k)
                s_scr[k, rows, :] = sn
                p = sn * row(r_ref, t, k)
                acc[k % parts] = p if acc[k % parts] is None else acc[k % parts] + p
            o_ref[t, rows, :] = tree(acc)
        return carry

    lax.fori_loop(0, tc, step, 0)

    @pl.when(c == pl.num_programs(2) - 1)
    def _():
        sf_ref[...] = s_scr[...]


def _wkv_scan(r, kk, v, w, kd, kka, s0, *, tc):
    lg, t, nk, _ = r.shape
    g = v.shape[2]
    nt = t // tc
    tmap = lambda d, l, c: c + d * (nt - 1 - 2 * c)
    shared = lambda rows: pl.BlockSpec((None, tc, rows, LANES), lambda d, l, c: (l, tmap(d, l, c), 0, 0))
    perdir = pl.BlockSpec((None, None, tc, nk, LANES), lambda d, l, c: (d, l, tmap(d, l, c), 0, 0))
    state = pl.BlockSpec((None, None, nk, g, LANES), lambda d, l, c: (d, l, 0, 0, 0))
    return pl.pallas_call(
        functools.partial(_scan_kernel, tc=tc, g=g),
        grid=(2, lg, nt),
        in_specs=[shared(nk), shared(nk), shared(g), perdir, perdir, perdir, state],
        out_specs=[pl.BlockSpec((None, None, tc, g, LANES), lambda d, l, c: (d, l, tmap(d, l, c), 0, 0)), state],
        out_shape=[jax.ShapeDtypeStruct((2, lg, t, g, LANES), F32),
                   jax.ShapeDtypeStruct((2, lg, nk, g, LANES), F32)],
        scratch_shapes=[pltpu.VMEM((nk, g, LANES), F32)],
        compiler_params=_cparams("parallel", "parallel", "arbitrary"),
        name="wkv_scan",
    )(r, kk, v, w, kd, kka, s0)


def _chain_layout(batch):
    chains = batch * A_HEADS
    if chains >= LANES:
        assert chains % LANES == 0
        return chains // LANES, 1
    assert LANES % chains == 0 and A_HEAD % (LANES // chains) == 0
    return 1, LANES // chains


def _to_lanes_k(x, batch, t, lg, vq):
    lead = x.shape[:-2]
    nl = len(lead)
    chains = (batch // lg) * A_HEADS
    x = x.reshape(lead + (lg, 1, batch // lg, t, A_HEADS, A_HEAD))
    x = jnp.broadcast_to(x, lead + (lg, vq, batch // lg, t, A_HEADS, A_HEAD))
    perm = tuple(range(nl)) + tuple(nl + p for p in (0, 3, 5, 1, 2, 4))
    return jnp.transpose(x, perm).reshape(lead + (lg, t, A_HEAD, vq * chains))


def _to_lanes_v(x, batch, t, lg, vq):
    gsz = A_HEAD // vq
    x = x.reshape(lg, batch // lg, t, A_HEADS, vq, gsz)
    x = jnp.transpose(x, (0, 2, 5, 4, 1, 3))
    return x.reshape(lg, t, gsz, vq * (batch // lg) * A_HEADS)


def _from_lanes_v(o, batch, t, lg, vq):
    lead = o.shape[:-4]
    gsz = A_HEAD // vq
    o = o.reshape(lead + (lg, t, gsz, vq, batch // lg, A_HEADS))
    nl = len(lead)
    perm = tuple(range(nl)) + tuple(nl + p for p in (0, 4, 1, 5, 3, 2))
    return jnp.transpose(o, perm).reshape(lead + (batch * t, D_HALF))


def _state_to_lanes(s, lg, vq):
    batch = s.shape[0]
    gsz = A_HEAD // vq
    s = s.reshape(lg, batch // lg, 2, A_HEADS, vq, gsz, A_HEAD)
    s = jnp.transpose(s, (2, 0, 6, 5, 4, 1, 3))
    return s.reshape(2, lg, A_HEAD, gsz, LANES)


def _state_from_lanes(s, batch, lg, vq):
    gsz = A_HEAD // vq
    s = s.reshape(2, lg, A_HEAD, gsz, vq, batch // lg, A_HEADS)
    s = jnp.transpose(s, (1, 5, 0, 6, 4, 3, 2))
    return s.reshape(batch, 2, A_HEADS, A_HEAD, A_HEAD)


def _even_post_kernel(x_ref, o_ref_in, bonus_ref, g_ref, bg_ref, z_ref, zprev_ref, znext_ref,
                      lng_ref, lnb_ref, cw_ref, ones_ref, gate_ref, w_ref, out_ref,
                      *, tm, np_tiles, seq_tiles, dec_tiles):
    i = pl.program_id(0)
    ones = ones_ref[...]
    inv = 1.0 / A_HEAD
    o = o_ref_in[0] + o_ref_in[1]
    mu = _dot_exact01(o, ones) * inv
    xc = o - mu
    var = _dot_exact01(xc * xc, ones) * inv
    on = xc * lax.rsqrt(var + GN_EPS) * lng_ref[...] + lnb_ref[...]
    ya = (on + bonus_ref[...]) * g_ref[...]

    pos = jnp.where(i < np_tiles, i % seq_tiles, (i - np_tiles) % dec_tiles)
    last = jnp.where(i < np_tiles, seq_tiles - 1, dec_tiles - 1)
    z = z_ref[...]
    rows = lax.broadcasted_iota(jnp.int32, z.shape, 0)
    prev_row = jnp.where(pos == 0, 0.0, zprev_ref[pl.ds(SUBLANES - 1, 1), :])
    next_row = jnp.where(pos == last, 0.0, znext_ref[pl.ds(0, 1), :])
    zp = jnp.where(rows == 0, prev_row, pltpu.roll(z, 1, axis=0))
    zn = jnp.where(rows == tm - 1, next_row, pltpu.roll(z, tm - 1, axis=0))
    yb = bg_ref[...] * (zp * cw_ref[pl.ds(0, 1), :] + z * cw_ref[pl.ds(1, 1), :] + zn * cw_ref[pl.ds(2, 1), :])

    h = D_HALF
    y = _dot(ya.astype(BF16), w_ref[pl.ds(0, h), :]) + _dot(yb.astype(BF16), w_ref[pl.ds(h, h), :])
    out_ref[...] = x_ref[...] + gate_ref[...] * y


def _even_post(x, o2, bonus, g, y, z, ln_g, ln_b, conv_w, gate, w_out, *, tm, n_prompt, seq, dec_seq):
    n, d = x.shape
    h = D_HALF
    nblk8 = n // SUBLANES
    per8 = tm // SUBLANES
    tok = pl.BlockSpec((tm, h), lambda i: (i, 0))
    full2 = lambda s: pl.BlockSpec(s, lambda i: (0, 0))
    mod_map = lambda i: (_mod_row(i, tm, n_prompt, dec_seq), 0, 0)
    return pl.pallas_call(
        functools.partial(_even_post_kernel, tm=tm, np_tiles=n_prompt // tm, seq_tiles=seq // tm,
                          dec_tiles=dec_seq // tm),
        grid=(n // tm,),
        in_specs=[
            pl.BlockSpec((tm, d), lambda i: (i, 0)),
            pl.BlockSpec((2, tm, h), lambda i: (0, i, 0)),
            tok, tok,
            pl.BlockSpec((tm, h), lambda i: (i, 3)),
            tok,
            pl.BlockSpec((SUBLANES, h), lambda i: (jnp.maximum(i * per8 - 1, 0), 0)),
            pl.BlockSpec((SUBLANES, h), lambda i: (jnp.minimum((i + 1) * per8, nblk8 - 1), 0)),
            full2((1, h)), full2((1, h)), full2((SUBLANES, h)), full2((h, h)),
            pl.BlockSpec((None, 1, d), mod_map),
            full2((d, d)),
        ],
        out_specs=pl.BlockSpec((tm, d), lambda i: (i, 0)),
        out_shape=jax.ShapeDtypeStruct((n, d), F32),
        compiler_params=_cparams("parallel"),
        name="even_post",
    )(x, o2, bonus, g, y, z, z, z, ln_g.reshape(1, h), ln_b.reshape(1, h),
      jnp.pad(conv_w, ((0, SUBLANES - conv_w.shape[0]), (0, 0))), _head_ones(), gate, w_out.astype(BF16))


def _attn_kernel(lam_ref, q_ref, k_ref, v_ref, subln_ref, o_ref, *, out_scale):
    lam = lam_ref[0]
    hd2 = 2 * C_HD
    for h in range(C_HEADS):
        parts = []
        for j in range(2):
            lo = h * hd2 + j * C_HD
            s = _dot(q_ref[:, lo:lo + C_HD], k_ref[:, lo:lo + C_HD], NT)
            e = jnp.exp(s - jnp.max(s, axis=-1, keepdims=True))
            inv = 1.0 / jnp.sum(e, axis=-1, keepdims=True)
            parts.append(_dot(e.astype(BF16), v_ref[:, h * hd2:(h + 1) * hd2]) * inv)
        oh = parts[0] - lam * parts[1]
        y = oh * lax.rsqrt(jnp.mean(oh * oh, axis=-1, keepdims=True) + NORM_EPS)
        o_ref[:, h * hd2:(h + 1) * hd2] = y * subln_ref[...] * out_scale


def _diff_attention(lam, q, k, v, subln, lam_init, *, batch, tq_total, tq):
    d = q.shape[1]
    tk = k.shape[1]
    nq = tq_total // tq
    return pl.pallas_call(
        functools.partial(_attn_kernel, out_scale=1.0 - lam_init),
        grid=(batch, nq),
        in_specs=[
            pl.BlockSpec(memory_space=pltpu.SMEM),
            pl.BlockSpec((tq, d), lambda b, i: (b * nq + i, 0)),
            pl.BlockSpec((None, tk, d), lambda b, i: (b, 0, 0)),
            pl.BlockSpec((None, tk, d), lambda b, i: (b, 0, 0)),
            pl.BlockSpec((1, 2 * C_HD), lambda b, i: (0, 0)),
        ],
        out_specs=pl.BlockSpec((tq, d), lambda b, i: (b * nq + i, 0)),
        out_shape=jax.ShapeDtypeStruct((batch * tq_total, d), F32),
        compiler_params=_cparams("parallel", "parallel"),
        name="diff_attn",
    )(lam.reshape(1), q, k, v, subln.reshape(1, 2 * C_HD))


def _axial_rope_tables(n_tok):
    rows = n_tok // GRID_W
    row = jnp.repeat(jnp.arange(rows, dtype=F32), GRID_W)
    col = jnp.tile(jnp.arange(GRID_W, dtype=F32), rows)
    nf = C_HD // 4
    inv = ROPE_BASE ** (-jnp.arange(nf, dtype=F32) / nf)
    ar = row[:, None] * inv[None, :]
    ac = col[:, None] * inv[None, :]
    ang = jnp.concatenate([ar, ar, ac, ac], axis=-1)
    return jnp.cos(ang), jnp.sin(ang)


def _apply_rope(x, cos, sin):
    q = C_HD // 4
    first = (jnp.arange(x.shape[-1]) % (2 * q)) < q
    xr = jnp.where(first, -jnp.roll(x, -q, axis=-1), jnp.roll(x, q, axis=-1))
    return x * cos[None] + xr * sin[None]


N_TOP = P_TOPK + 1
CAND_PAIRS = [(i, j) for i in range(N_TOP) for j in range(N_TOP) if (i + 1) * (j + 1) <= N_TOP]
CAND_ROWS = -(-len(CAND_PAIRS) // SUBLANES) * SUBLANES
TOP_ROWS = -(-N_TOP // SUBLANES) * SUBLANES


def _take_top(x, n, emit, want_rank=False):
    rank = jnp.full(x.shape, float(n), F32) if want_rank else None
    for it in range(n):
        m = jnp.max(x, axis=0, keepdims=True)
        emit(it, m)
        hit = x == m
        if want_rank:
            rank = jnp.where(hit, float(it), rank)
        if it + 1 < n:
            x = jnp.where(hit, NEG_INF, x)
    return rank


def _pack_bf16_rows(x):
    return pltpu.bitcast(x.astype(BF16), jnp.int32)


def _peer_select_kernel(q_ref, skh_ref, skl_ref, l_out, c1_out, r2_out, e2_out, top_scr, cand_scr):
    half = P_QDIM // 2
    for h in range(P_HEADS):
        sT = []
        for j in range(2):
            hj = 2 * h + j
            qs = q_ref[:, hj * half:(hj + 1) * half]
            q_hi, q_lo = _split2(qs)
            s = _dot(skh_ref[hj], q_hi, NT) + (_dot(skh_ref[hj], q_lo, NT) + _dot(skl_ref[hj], q_hi, NT))
            sT.append(s)

            def emit(it, m, j=j):
                top_scr[j, pl.ds(it, 1), :] = m
            rank = _take_top(s, N_TOP, emit, want_rank=(j == 1))

        cand_scr[...] = jnp.full(cand_scr.shape, NEG_INF, F32)
        for c, (i, j) in enumerate(CAND_PAIRS):
            cand_scr[pl.ds(c, 1), :] = top_scr[0, pl.ds(i, 1), :] + top_scr[1, pl.ds(j, 1), :]
        a1 = top_scr[0, pl.ds(0, 1), :]
        b1 = top_scr[1, pl.ds(0, 1), :]
        mx = a1 + b1
        st = {"z": None, "c16": None, "c17": None}

        def emit_c(it, m):
            if it < P_TOPK:
                e = jnp.exp(m - mx)
                st["z"] = e if st["z"] is None else st["z"] + e
            if it == P_TOPK - 1:
                st["c16"] = m
            if it == P_TOPK:
                st["c17"] = m
        _take_top(cand_scr[...], N_TOP, emit_c)
        tau = 0.5 * (st["c16"] + st["c17"])
        thr = tau - sT[0]
        cnt = jnp.zeros(thr.shape, F32)
        for k in range(N_TOP):
            cnt = cnt + jnp.where(top_scr[1, pl.ds(k, 1), :] >= thr, 1.0, 0.0)
        l_out[h] = cnt
        c1_out[h] = jnp.exp(sT[0] - a1) / st["z"]
        e2 = jnp.exp(sT[1] - b1)
        for g in range(e2.shape[1] // LANES):
            r2_out[h, g] = _pack_bf16_rows(rank[:, g * LANES:(g + 1) * LANES])
            e2_out[h, g] = _pack_bf16_rows(e2[:, g * LANES:(g + 1) * LANES])


def _peer_select(q, sub_keys, *, tm):
    n = q.shape[0]
    sk = sub_keys.reshape(2 * P_HEADS, P_NKEYS, P_QDIM // 2)
    skh, skl = _split2(sk)
    out = pl.BlockSpec((P_HEADS, P_NKEYS, tm), lambda i: (0, 0, i))
    sds = jax.ShapeDtypeStruct((P_HEADS, P_NKEYS, n), F32)
    out_g = pl.BlockSpec((P_HEADS, tm // LANES, P_NKEYS // 2, LANES), lambda i: (0, i, 0, 0))
    sds_g = jax.ShapeDtypeStruct((P_HEADS, n // LANES, P_NKEYS // 2, LANES), jnp.int32)
    return pl.pallas_call(
        _peer_select_kernel,
        grid=(n // tm,),
        in_specs=[pl.BlockSpec((tm, P_HEADS * P_QDIM), lambda i: (i, 0)),
                  pl.BlockSpec(sk.shape, lambda i: (0, 0, 0)),
                  pl.BlockSpec(sk.shape, lambda i: (0, 0, 0))],
        out_specs=[out, out, out_g, out_g],
        out_shape=[sds, sds, sds_g, sds_g],
        scratch_shapes=[pltpu.VMEM((2, TOP_ROWS, tm), F32), pltpu.VMEM((CAND_ROWS, tm), F32)],
        compiler_params=_cparams("parallel"),
        name="peer_select",
    )(q, skh, skl)


PAIR = 2
PAIR_ROWS = PAIR * P_NKEYS
I2_HALF = P_NKEYS // 2


def _gelu(x):
    return 0.5 * x * (1.0 + lax.erf(x * (1.0 / math.sqrt(2.0))))


def _peer_dense_kernel(hbt_ref, x_ref, gate_ref, ufirst_ref, unext_ref, vtprev_ref, vtlast_ref,
                       l_ref, c1_ref, r2_ref, e2_ref, o_ref, acc_ref, a_even, a_odd, act_even, act_odd,
                       *, stages, tm):
    c = pl.program_id(1)

    def put_act(act_ref, act):
        for q in range(stages):
            for lg in range(tm // LANES):
                act_ref[q, lg] = act[q * PAIR_ROWS:(q + 1) * PAIR_ROWS, lg * LANES:(lg + 1) * LANES]

    @pl.when(c == 0)
    def _():
        acc_ref[...] = jnp.zeros(acc_ref.shape, F32)
        a_odd[...] = jnp.zeros(a_odd.shape, BF16)
        put_act(act_even, _gelu(_dot(ufirst_ref[...], hbt_ref[...])))

    def chunk(act_cur, act_next, a_cur, a_prev):
        put_act(act_next, _gelu(_dot(unext_ref[...], hbt_ref[...])))
        for p in range(stages):
            for lg in range(tm // LANES):
                ls = slice(lg * LANES, (lg + 1) * LANES)
                for half in range(P_NKEYS // I2_HALF):
                    words = slice(half * I2_HALF // 2, (half + 1) * I2_HALF // 2)
                    w = [None] * PAIR
                    for h in range(P_HEADS):
                        r2 = pltpu.bitcast(r2_ref[h, lg, words, :], BF16)
                        e2 = pltpu.bitcast(e2_ref[h, lg, words, :], BF16)
                        for j in range(PAIR):
                            row = pl.ds(PAIR * p + j, 1)
                            cnt = jnp.broadcast_to(l_ref[h, row, ls], (I2_HALF, LANES)).astype(BF16)
                            c1 = jnp.broadcast_to(c1_ref[h, row, ls], (I2_HALF, LANES)).astype(BF16)
                            term = jnp.where(r2 < cnt, e2 * c1, jnp.zeros_like(e2))
                            w[j] = term if w[j] is None else w[j] + term
                    for j in range(PAIR):
                        off = j * P_NKEYS + half * I2_HALF
                        rs = slice(p * PAIR_ROWS + off, p * PAIR_ROWS + off + I2_HALF)
                        a_cur[rs, ls] = w[j] * act_cur[p, lg, off:off + I2_HALF, :].astype(BF16)
        acc_ref[...] += _dot(vtprev_ref[...], a_prev[...])

    @pl.when(c % 2 == 0)
    def _():
        chunk(act_even, act_odd, a_even, a_odd)

    @pl.when(c % 2 == 1)
    def _():
        chunk(act_odd, act_even, a_odd, a_even)

    @pl.when(c == pl.num_programs(1) - 1)
    def _():
        total = acc_ref[...] + _dot(vtlast_ref[...], a_odd[...])
        o_ref[...] = x_ref[...] + gate_ref[...] * total.T


def _peer_dense(x, hbt, gate, u, v, cnt, c1, r2, e2, *, tm, stages, n_prompt, dec_seq):
    n, d = x.shape
    ec = stages * PAIR_ROWS
    n_chunks = P_EXPERTS // ec
    assert n_chunks % 2 == 0
    u_bf = u.astype(BF16)
    vt_bf = v.astype(BF16).T
    mod_map = lambda i, c: (_mod_row(i, tm, n_prompt, dec_seq), 0, 0)
    small = pl.BlockSpec((P_HEADS, stages * PAIR, tm), lambda i, c: (0, c, i))
    big = pl.BlockSpec((P_HEADS, tm // LANES, P_NKEYS // 2, LANES), lambda i, c: (0, i, 0, 0))
    act_shape = (stages, tm // LANES, PAIR_ROWS, LANES)
    return pl.pallas_call(
        functools.partial(_peer_dense_kernel, stages=stages, tm=tm),
        grid=(n // tm, n_chunks),
        in_specs=[
            pl.BlockSpec((d, tm), lambda i, c: (0, i)),
            pl.BlockSpec((tm, d), lambda i, c: (i, 0)),
            pl.BlockSpec((None, 1, d), mod_map),
            pl.BlockSpec((ec, d), lambda i, c: (0, 0)),
            pl.BlockSpec((ec, d), lambda i, c: (jnp.minimum(c + 1, n_chunks - 1), 0)),
            pl.BlockSpec((d, ec), lambda i, c: (0, jnp.maximum(c - 1, 0))),
            pl.BlockSpec((d, ec), lambda i, c: (0, n_chunks - 1)),
            small, small, big, big,
        ],
        out_specs=pl.BlockSpec((tm, d), lambda i, c: (i, 0)),
        out_shape=jax.ShapeDtypeStruct((n, d), F32),
        scratch_shapes=[pltpu.VMEM((d, tm), F32), pltpu.VMEM((ec, tm), BF16), pltpu.VMEM((ec, tm), BF16),
                        pltpu.VMEM(act_shape, F32), pltpu.VMEM(act_shape, F32)],
        compiler_params=_cparams("parallel", "arbitrary"),
        name="peer_dense",
    )(hbt, x, gate, u_bf, u_bf, vt_bf, vt_bf, cnt, c1, r2, e2)


def _final_norm_kernel(x_ref, g_ref, o_ref):
    x = x_ref[...]
    o_ref[...] = x * lax.rsqrt(jnp.mean(x * x, axis=-1, keepdims=True) + NORM_EPS) * g_ref[...]


def _final_norm(x, g, *, tm):
    n, d = x.shape
    return pl.pallas_call(
        _final_norm_kernel,
        grid=(n // tm,),
        in_specs=[pl.BlockSpec((tm, d), lambda i: (i, 0)), pl.BlockSpec((1, d), lambda i: (0, 0))],
        out_specs=pl.BlockSpec((tm, d), lambda i: (i, 0)),
        out_shape=jax.ShapeDtypeStruct((n, d), F32),
        compiler_params=_cparams("parallel"),
        name="final_norm",
    )(x, g.reshape(1, d))


TOKEN_TILE = 256
DENSE_TILE = 512


def _tiles(n_prompt, seq, dec_seq):
    tm = math.gcd(math.gcd(seq, dec_seq), TOKEN_TILE)
    tm_dense = math.gcd(math.gcd(n_prompt, dec_seq), DENSE_TILE)
    return tm, tm_dense


def kernel(x_prompt, x_sample, c, state_rwkv, cache_k, cache_v, c_ctx, w_ada, b_ada, norm_mix, norm_ffn, w_in_even, rwkv_w0, rwkv_w_up, rwkv_a0, rwkv_a_up, rwkv_g_up, rwkv_k_k, rwkv_k_a, rwkv_r_k, rwkv_ln_g, rwkv_ln_b, conv_w, w_out_even, w_in_odd, diff_lam, diff_subln, w_out_odd, peer_w_q, peer_sub_keys, peer_u, peer_v, final_norm):
    bp, seq, d = x_prompt.shape
    bs, dec_seq, _ = x_sample.shape
    depth = w_ada.shape[0]
    n_prompt = bp * seq
    n_sample = bs * dec_seq
    tm, tm_dense = _tiles(n_prompt, seq, dec_seq)
    tok = dict(n_prompt=n_prompt, dec_seq=dec_seq)

    x = jnp.concatenate([x_prompt.reshape(n_prompt, d), x_sample.reshape(n_sample, d)], axis=0)

    rows = 1 + bs
    rows_pad = -(-rows // SUBLANES) * SUBLANES
    cond = jnp.zeros((rows_pad, d), F32).at[0].set(c_ctx).at[1:rows].set(c)
    mod = _adaln_all(cond, w_ada, b_ada)[:, :rows].reshape(depth, rows, 6, 1, d)
    mods = [[mod[l, :, m] for m in range(6)] for l in range(depth)]

    h = D_HALF
    perm = jnp.concatenate([jnp.arange(0, 3 * h), jnp.arange(3 * h + LORA_W, 6 * h + LORA_W),
                            jnp.arange(3 * h, 3 * h + LORA_W)])

    lg_p, vq_p = _chain_layout(bp)
    lg_s, vq_s = _chain_layout(bs)
    cos, sin = (jnp.tile(tab, (1, d // C_HD)) for tab in _axial_rope_tables(dec_seq))
    scale = C_HD ** -0.5

    new_s, new_k, new_v = [], [], []
    for l in range(depth):
        i = l // 2
        sh1, sc1, g1, sh2, sc2, g2 = mods[l]
        if l % 2 == 0:
            y = _norm_mod_matmul(x, norm_mix[l], sc1, sh1, w_in_even[i][:, perm], tm=tm, passes=1, name="in_even", **tok)
            kk, w, kd, kka, bonus, g, z = _rwkv_prep(
                y, rwkv_w0[i], rwkv_w_up[i], rwkv_a0[i], rwkv_a_up[i], rwkv_g_up[i],
                rwkv_k_k[i], rwkv_k_a[i], rwkv_r_k[i].reshape(-1), tm=tm)
            r = y[:, 0:h]
            v = y[:, 2 * h:3 * h]
            outs = []
            for (lo, n_tok, batch, t, lg, vq, s0) in (
                    (0, n_prompt, bp, seq, lg_p, vq_p, None),
                    (n_prompt, n_sample, bs, dec_seq, lg_s, vq_s, state_rwkv[:, i])):
                sl = slice(lo, lo + n_tok)
                lk = lambda a: _to_lanes_k(a, batch, t, lg, vq)
                gsz = A_HEAD // vq
                s0l = (jnp.zeros((2, lg, A_HEAD, gsz, LANES), F32) if s0 is None
                       else _state_to_lanes(s0, lg, vq))
                o, sf = _wkv_scan(lk(r[sl]), lk(kk[sl]), _to_lanes_v(v[sl], batch, t, lg, vq),
                                  lk(w[:, sl]), lk(kd[:, sl]), lk(kka[:, sl]), s0l, tc=min(16, t))
                outs.append(_from_lanes_v(o, batch, t, lg, vq))
                if s0 is None:
                    new_s.append(_state_from_lanes(sf, batch, lg, vq))
            o2 = jnp.concatenate(outs, axis=1)
            x = _even_post(x, o2, bonus, g, y, z, rwkv_ln_g[i], rwkv_ln_b[i], conv_w[i], g1, w_out_even[i],
                           tm=tm, n_prompt=n_prompt, seq=seq, dec_seq=dec_seq)
        else:
            lam_init = 0.8 - 0.6 * math.exp(-0.3 * l)
            lp = diff_lam[i].astype(F32)
            lam = jnp.exp(jnp.sum(lp[0] * lp[1])) - jnp.exp(jnp.sum(lp[2] * lp[3])) + lam_init
            y = _norm_mod_matmul(x, norm_mix[l], sc1, sh1, w_in_odd[i], tm=tm, passes=1, name="in_odd", **tok)
            q, k, v = y[:, 0:d], y[:, d:2 * d], y[:, 2 * d:3 * d]
            new_k.append(k[:n_prompt].reshape(bp, seq, C_HEADS, 2, C_HD))
            new_v.append(v[:n_prompt].reshape(bp, seq, C_HEADS, 2 * C_HD))
            o_p = _diff_attention(lam, (q[:n_prompt] * scale).astype(BF16),
                                  k[:n_prompt].reshape(bp, seq, d).astype(BF16),
                                  v[:n_prompt].reshape(bp, seq, d).astype(BF16),
                                  diff_subln[i], lam_init, batch=bp, tq_total=seq, tq=min(seq, 256))
            shp = (bs, dec_seq, d)
            qs = _apply_rope(q[n_prompt:].reshape(shp), cos, sin).reshape(n_sample, d)
            ks = _apply_rope(k[n_prompt:].reshape(shp), cos, sin)
            k_all = jnp.concatenate([ks, cache_k[:, i].reshape(bs, -1, d)], axis=1).astype(BF16)
            v_all = jnp.concatenate([v[n_prompt:].reshape(bs, dec_seq, d), cache_v[:, i].reshape(bs, -1, d)],
                                    axis=1).astype(BF16)
            o_s = _diff_attention(lam, (qs * scale).astype(BF16), k_all, v_all, diff_subln[i], lam_init,
                                  batch=bs, tq_total=dec_seq, tq=min(dec_seq, 256))
            x = _proj_residual(x, jnp.concatenate([o_p, o_s], axis=0), g1, w_out_odd[i], tm=tm,
                               name="out_odd", **tok)

        qp, hbt = _norm_mod_matmul(x, norm_ffn[l], sc2, sh2, peer_w_q[l], tm=tm, passes=1, emit_h=True,
                                  name="peer_q", **tok)
        cnt_, c1_, r2_, e2_ = _peer_select(qp, peer_sub_keys[l], tm=tm)
        x = _peer_dense(x, hbt, g2, peer_u[l], peer_v[l], cnt_, c1_, r2_, e2_, tm=tm_dense, stages=4, **tok)

    yfin = _final_norm(x, final_norm, tm=tm)
    y_prompt = yfin[:n_prompt].reshape(bp, seq, d)
    y_sample = yfin[n_prompt:].reshape(bs, dec_seq, d)
    return (y_prompt, y_sample, jnp.stack(new_s, axis=1), jnp.stack(new_k, axis=1), jnp.stack(new_v, axis=1))
```

```python
import functools
import math

import jax
import jax.numpy as jnp
from jax import lax
from jax.experimental import pallas as pl
from jax.experimental.pallas import tpu as pltpu

D_MODEL = 1024
GRID_W = 64
ROPE_BASE = 10000.0
NORM_EPS = 1e-6
D_HALF = D_MODEL // 2
A_HEAD = 64
A_HEADS = D_HALF // A_HEAD
A_DECAY_LORA = 64
A_ICLR_LORA = 64
A_GATE_LORA = 128
GN_EPS = 64e-5
C_HEADS = 8
C_HD = D_MODEL // C_HEADS // 2
P_HEADS = 8
P_NKEYS = 128
P_EXPERTS = P_NKEYS * P_NKEYS
P_QDIM = 256
P_TOPK = 16
LORA_W = A_DECAY_LORA + A_ICLR_LORA + A_GATE_LORA
EVEN_IN = 6 * D_HALF + LORA_W

LANES = 128
SUBLANES = 8
VMEM_LIMIT = 56 * 1024 * 1024

F32 = jnp.float32
BF16 = jnp.bfloat16
NN = (((1,), (0,)), ((), ()))
NT = (((1,), (1,)), ((), ()))
NEG_INF = float("-inf")


def _cparams(*sem):
    return pltpu.CompilerParams(dimension_semantics=sem, vmem_limit_bytes=VMEM_LIMIT)


def _split2(a):
    hi = a.astype(BF16)
    lo = (a - hi.astype(F32)).astype(BF16)
    return hi, lo


def _split3(a):
    hi = a.astype(BF16)
    r = a - hi.astype(F32)
    mid = r.astype(BF16)
    lo = (r - mid.astype(F32)).astype(BF16)
    return hi, mid, lo


def _dot(a, b, dims=NN):
    return lax.dot_general(a, b, dims, preferred_element_type=F32)


def _dot3(a, b_hi, b_lo, dims=NN):
    a_hi, a_lo = _split2(a)
    return _dot(a_hi, b_hi, dims) + (_dot(a_hi, b_lo, dims) + _dot(a_lo, b_hi, dims))


def _dot_exact01(a, ones_bf16):
    hi, mid, lo = _split3(a)
    return _dot(hi, ones_bf16) + (_dot(mid, ones_bf16) + _dot(lo, ones_bf16))


def _mod_row(i, tm, n_prompt, dec_seq):
    np_tiles = n_prompt // tm
    per = dec_seq // tm
    return jnp.where(i < np_tiles, 0, 1 + (i - np_tiles) // per)


def _ada_kernel(c_ref, whi_ref, wlo_ref, b_ref, o_ref):
    c = c_ref[...]
    s = c * jax.nn.sigmoid(c)
    o_ref[0] = _dot3(s, whi_ref[0], wlo_ref[0]) + b_ref[0]


def _adaln_all(cond, w_ada, b_ada):
    depth, d, n6 = w_ada.shape
    rows = cond.shape[0]
    tn = n6 // 4
    whi, wlo = _split2(w_ada)
    return pl.pallas_call(
        _ada_kernel,
        grid=(depth, n6 // tn),
        in_specs=[
            pl.BlockSpec((rows, d), lambda l, j: (0, 0)),
            pl.BlockSpec((1, d, tn), lambda l, j: (l, 0, j)),
            pl.BlockSpec((1, d, tn), lambda l, j: (l, 0, j)),
            pl.BlockSpec((1, 1, tn), lambda l, j: (l, 0, j)),
        ],
        out_specs=pl.BlockSpec((1, rows, tn), lambda l, j: (l, 0, j)),
        out_shape=jax.ShapeDtypeStruct((depth, rows, n6), F32),
        compiler_params=_cparams("parallel", "parallel"),
        name="adaln",
    )(cond, whi, wlo, b_ada.reshape(depth, 1, n6))


def _nmm_kernel(x_ref, g_ref, sc_ref, sh_ref, w_ref, o_ref):
    x = x_ref[...]
    y = x * lax.rsqrt(jnp.mean(x * x, axis=-1, keepdims=True) + NORM_EPS)
    h = y * g_ref[...] * (1.0 + sc_ref[...]) + sh_ref[...]
    o_ref[...] = _dot(h.astype(BF16), w_ref[...])


def _norm_mod_matmul(x, gnorm, scale, shift, w, *, tm, n_prompt, dec_seq, name):
    n, d = x.shape
    nout = w.shape[1]
    tn = nout // 2
    mod_map = lambda j, i: (_mod_row(i, tm, n_prompt, dec_seq), 0, 0)
    return pl.pallas_call(
        _nmm_kernel,
        grid=(nout // tn, n // tm),
        in_specs=[
            pl.BlockSpec((tm, d), lambda j, i: (i, 0)),
            pl.BlockSpec((1, d), lambda j, i: (0, 0)),
            pl.BlockSpec((None, 1, d), mod_map),
            pl.BlockSpec((None, 1, d), mod_map),
            pl.BlockSpec((d, tn), lambda j, i: (0, j)),
        ],
        out_specs=pl.BlockSpec((tm, tn), lambda j, i: (i, j)),
        out_shape=jax.ShapeDtypeStruct((n, nout), F32),
        compiler_params=_cparams("arbitrary", "arbitrary"),
        name=name,
    )(x, gnorm.reshape(1, d), scale, shift, w.astype(BF16))


def _proj_res_kernel(x_ref, a_ref, gate_ref, w_ref, o_ref):
    o_ref[...] = x_ref[...] + gate_ref[...] * _dot(a_ref[...].astype(BF16), w_ref[...])


def _proj_residual(x, a, gate, w, *, tm, n_prompt, dec_seq, name):
    n, d = x.shape
    k = a.shape[1]
    mod_map = lambda i: (_mod_row(i, tm, n_prompt, dec_seq), 0, 0)
    return pl.pallas_call(
        _proj_res_kernel,
        grid=(n // tm,),
        in_specs=[
            pl.BlockSpec((tm, d), lambda i: (i, 0)),
            pl.BlockSpec((tm, k), lambda i: (i, 0)),
            pl.BlockSpec((None, 1, d), mod_map),
            pl.BlockSpec((k, d), lambda i: (0, 0)),
        ],
        out_specs=pl.BlockSpec((tm, d), lambda i: (i, 0)),
        out_shape=jax.ShapeDtypeStruct((n, d), F32),
        compiler_params=_cparams("parallel"),
        name=name,
    )(x, a, gate, w.astype(BF16))


def _rwkv_prep_kernel(r_ref, k_ref, v_ref, cg_ref, hc_ref, lora_ref,
                      w0_ref, wuph_ref, wupl_ref, a0_ref, auph_ref, aupl_ref, guph_ref, gupl_ref,
                      kk_ref_p, ka_ref_p, rk_ref_p, ones_ref,
                      kk_out, w_out, kd_out, kka_out, bonus_out, g_out, z_out):
    r = r_ref[...]
    k = k_ref[...]
    v = v_ref[...]
    lora = lora_ref[...]
    ones = ones_ref[...]
    wd = lora[:, :A_DECAY_LORA]
    ad = lora[:, A_DECAY_LORA:A_DECAY_LORA + A_ICLR_LORA]
    gd = lora[:, A_DECAY_LORA + A_ICLR_LORA:]

    kk = k * kk_ref_p[...]
    nrm = jnp.sqrt(_dot_exact01(kk * kk, ones))
    kk = kk / jnp.maximum(nrm, 1e-12)
    kk_out[...] = kk

    tw = jnp.tanh(wd)
    bonus = None
    for d in range(2):
        w_pre = w0_ref[d] + _dot3(tw, wuph_ref[d], wupl_ref[d])
        w_out[d] = jnp.exp(-(math.exp(-0.5) * jax.nn.sigmoid(w_pre)))
        a = jax.nn.sigmoid(a0_ref[d] + _dot3(ad, auph_ref[d], aupl_ref[d]))
        kd = k * (1.0 + (a - 1.0) * ka_ref_p[...])
        kd_out[d] = kd
        kka_out[d] = kk * a
        b = _dot_exact01(r * kd * rk_ref_p[...], ones) * v
        bonus = b if bonus is None else bonus + b
    bonus_out[...] = bonus
    g_out[...] = _dot3(jax.nn.sigmoid(gd), guph_ref[...], gupl_ref[...])
    z_out[...] = cg_ref[...] * hc_ref[...]


def _head_ones():
    idx = jnp.arange(D_HALF) // A_HEAD
    return (idx[:, None] == idx[None, :]).astype(BF16)


def _rwkv_prep(y, w0, w_up, a0, a_up, g_up, k_k, k_a, r_k, *, tm):
    n = y.shape[0]
    h = D_HALF
    wuph, wupl = _split2(w_up)
    auph, aupl = _split2(a_up)
    guph, gupl = _split2(g_up)
    col = lambda c: pl.BlockSpec((tm, h), lambda i: (i, c))
    full2 = lambda s: pl.BlockSpec(s, lambda i: (0, 0))
    full3 = lambda s: pl.BlockSpec(s, lambda i: (0, 0, 0))
    tok = pl.BlockSpec((tm, h), lambda i: (i, 0))
    tok2 = pl.BlockSpec((2, tm, h), lambda i: (0, i, 0))
    sds = lambda *s: jax.ShapeDtypeStruct(s, F32)
    return pl.pallas_call(
        _rwkv_prep_kernel,
        grid=(n // tm,),
        in_specs=[col(0), col(1), col(2), col(4), col(5),
                  pl.BlockSpec((tm, LORA_W), lambda i: (i, 6 * h // LORA_W)),
                  full3((2, 1, h)), full3((2, A_DECAY_LORA, h)), full3((2, A_DECAY_LORA, h)),
                  full3((2, 1, h)), full3((2, A_ICLR_LORA, h)), full3((2, A_ICLR_LORA, h)),
                  full2((A_GATE_LORA, h)), full2((A_GATE_LORA, h)),
                  full2((1, h)), full2((1, h)), full2((1, h)), full2((h, h))],
        out_specs=[tok, tok2, tok2, tok2, tok, tok, tok],
        out_shape=[sds(n, h), sds(2, n, h), sds(2, n, h), sds(2, n, h), sds(n, h), sds(n, h), sds(n, h)],
        compiler_params=_cparams("parallel"),
        name="rwkv_prep",
    )(y, y, y, y, y, y,
      w0.reshape(2, 1, h), wuph, wupl, a0.reshape(2, 1, h), auph, aupl, guph, gupl,
      k_k.reshape(1, h), k_a.reshape(1, h), r_k.reshape(1, h), _head_ones())


SCAN_ROW_BLOCK = 32
SCAN_CHUNK = 16


def _scan_kernel(r_ref, kk_ref, v_ref, w_ref, kd_ref, kka_ref, s0_ref, o_ref, sf_ref, s_scr, *, tc, g):
    d = pl.program_id(0)
    c = pl.program_id(2)
    nk = A_HEAD
    parts = 4

    @pl.when(c == 0)
    def _():
        s_scr[...] = s0_ref[...]

    gb = min(g, SCAN_ROW_BLOCK)

    def row(ref, t, k):
        return jnp.broadcast_to(ref[t, pl.ds(k, 1), :], (gb, LANES))

    def tree(xs):
        while len(xs) > 1:
            xs = [xs[i] + xs[i + 1] for i in range(0, len(xs) - 1, 2)] + ([xs[-1]] if len(xs) % 2 else [])
        return xs[0]

    def step(i, carry):
        t = i + d * (tc - 1 - 2 * i)
        for g0 in range(0, g, gb):
            rows = slice(g0, g0 + gb)
            acc = [None] * parts
            for k in range(nk):
                p = s_scr[k, rows, :] * row(kk_ref, t, k)
                acc[k % parts] = p if acc[k % parts] is None else acc[k % parts] + p
            sa = tree(acc)
            vv = v_ref[t, rows, :]
            acc = [None] * parts
            for k in range(nk):
                sn = s_scr[k, rows, :] * row(w_ref, t, k) - sa * row(kka_ref, t, k) + vv * row(kd_ref, t, k)
                s_scr[k, rows, :] = sn
                p = sn * row(r_ref, t, k)
                acc[k % parts] = p if acc[k % parts] is None else acc[k % parts] + p
            o_ref[t, rows, :] = tree(acc)
        return carry

    lax.fori_loop(0, tc, step, 0)

    @pl.when(c == pl.num_programs(2) - 1)
    def _():
        sf_ref[...] = s_scr[...]


def _wkv_scan(r, kk, v, w, kd, kka, s0, *, tc):
    lg, t, nk, _ = r.shape
    g = v.shape[2]
    nt = t // tc
    tmap = lambda d, l, c: c + d * (nt - 1 - 2 * c)
    shared = lambda rows: pl.BlockSpec((None, tc, rows, LANES), lambda d, l, c: (l, tmap(d, l, c), 0, 0))
    perdir = pl.BlockSpec((None, None, tc, nk, LANES), lambda d, l, c: (d, l, tmap(d, l, c), 0, 0))
    state = pl.BlockSpec((None, None, nk, g, LANES), lambda d, l, c: (d, l, 0, 0, 0))
    return pl.pallas_call(
        functools.partial(_scan_kernel, tc=tc, g=g),
        grid=(2, lg, nt),
        in_specs=[shared(nk), shared(nk), shared(g), perdir, perdir, perdir, state],
        out_specs=[pl.BlockSpec((None, None, tc, g, LANES), lambda d, l, c: (d, l, tmap(d, l, c), 0, 0)), state],
        out_shape=[jax.ShapeDtypeStruct((2, lg, t, g, LANES), F32),
                   jax.ShapeDtypeStruct((2, lg, nk, g, LANES), F32)],
        scratch_shapes=[pltpu.VMEM((nk, g, LANES), F32)],
        compiler_params=_cparams("parallel", "parallel", "arbitrary"),
        name="wkv_scan",
    )(r, kk, v, w, kd, kka, s0)


def _chain_layout(batch):
    chains = batch * A_HEADS
    if chains >= LANES:
        assert chains % LANES == 0
        return chains // LANES, 1
    assert LANES % chains == 0 and A_HEAD % (LANES // chains) == 0
    return 1, LANES // chains


def _to_lanes_k(x, batch, t, lg, vq):
    lead = x.shape[:-2]
    nl = len(lead)
    chains = (batch // lg) * A_HEADS
    x = x.reshape(lead + (lg, 1, batch // lg, t, A_HEADS, A_HEAD))
    x = jnp.broadcast_to(x, lead + (lg, vq, batch // lg, t, A_HEADS, A_HEAD))
    perm = tuple(range(nl)) + tuple(nl + p for p in (0, 3, 5, 1, 2, 4))
    return jnp.transpose(x, perm).reshape(lead + (lg, t, A_HEAD, vq * chains))


def _to_lanes_v(x, batch, t, lg, vq):
    gsz = A_HEAD // vq
    x = x.reshape(lg, batch // lg, t, A_HEADS, vq, gsz)
    x = jnp.transpose(x, (0, 2, 5, 4, 1, 3))
    return x.reshape(lg, t, gsz, vq * (batch // lg) * A_HEADS)


def _from_lanes_v(o, batch, t, lg, vq):
    lead = o.shape[:-4]
    gsz = A_HEAD // vq
    o = o.reshape(lead + (lg, t, gsz, vq, batch // lg, A_HEADS))
    nl = len(lead)
    perm = tuple(range(nl)) + tuple(nl + p for p in (0, 4, 1, 5, 3, 2))
    return jnp.transpose(o, perm).reshape(lead + (batch * t, D_HALF))


def _state_to_lanes(s, lg, vq):
    batch = s.shape[0]
    gsz = A_HEAD // vq
    s = s.reshape(lg, batch // lg, 2, A_HEADS, vq, gsz, A_HEAD)
    s = jnp.transpose(s, (2, 0, 6, 5, 4, 1, 3))
    return s.reshape(2, lg, A_HEAD, gsz, LANES)


def _state_from_lanes(s, batch, lg, vq):
    gsz = A_HEAD // vq
    s = s.reshape(2, lg, A_HEAD, gsz, vq, batch // lg, A_HEADS)
    s = jnp.transpose(s, (1, 5, 0, 6, 4, 3, 2))
    return s.reshape(batch, 2, A_HEADS, A_HEAD, A_HEAD)


def _even_post_kernel(x_ref, o_ref_in, bonus_ref, g_ref, bg_ref, z_ref, zprev_ref, znext_ref,
                      lng_ref, lnb_ref, cw_ref, ones_ref, gate_ref, w_ref, out_ref,
                      *, tm, np_tiles, seq_tiles, dec_tiles):
    i = pl.program_id(0)
    ones = ones_ref[...]
    inv = 1.0 / A_HEAD
    o = o_ref_in[0] + o_ref_in[1]
    mu = _dot_exact01(o, ones) * inv
    xc = o - mu
    var = _dot_exact01(xc * xc, ones) * inv
    on = xc * lax.rsqrt(var + GN_EPS) * lng_ref[...] + lnb_ref[...]
    ya = (on + bonus_ref[...]) * g_ref[...]

    pos = jnp.where(i < np_tiles, i % seq_tiles, (i - np_tiles) % dec_tiles)
    last = jnp.where(i < np_tiles, seq_tiles - 1, dec_tiles - 1)
    z = z_ref[...]
    rows = lax.broadcasted_iota(jnp.int32, z.shape, 0)
    prev_row = jnp.where(pos == 0, 0.0, zprev_ref[pl.ds(SUBLANES - 1, 1), :])
    next_row = jnp.where(pos == last, 0.0, znext_ref[pl.ds(0, 1), :])
    zp = jnp.where(rows == 0, prev_row, pltpu.roll(z, 1, axis=0))
    zn = jnp.where(rows == tm - 1, next_row, pltpu.roll(z, tm - 1, axis=0))
    yb = bg_ref[...] * (zp * cw_ref[pl.ds(0, 1), :] + z * cw_ref[pl.ds(1, 1), :] + zn * cw_ref[pl.ds(2, 1), :])

    h = D_HALF
    y = _dot(ya.astype(BF16), w_ref[pl.ds(0, h), :]) + _dot(yb.astype(BF16), w_ref[pl.ds(h, h), :])
    out_ref[...] = x_ref[...] + gate_ref[...] * y


def _even_post(x, o2, bonus, g, y, z, ln_g, ln_b, conv_w, gate, w_out, *, tm, n_prompt, seq, dec_seq):
    n, d = x.shape
    h = D_HALF
    nblk8 = n // SUBLANES
    per8 = tm // SUBLANES
    tok = pl.BlockSpec((tm, h), lambda i: (i, 0))
    full2 = lambda s: pl.BlockSpec(s, lambda i: (0, 0))
    mod_map = lambda i: (_mod_row(i, tm, n_prompt, dec_seq), 0, 0)
    return pl.pallas_call(
        functools.partial(_even_post_kernel, tm=tm, np_tiles=n_prompt // tm, seq_tiles=seq // tm,
                          dec_tiles=dec_seq // tm),
        grid=(n // tm,),
        in_specs=[
            pl.BlockSpec((tm, d), lambda i: (i, 0)),
            pl.BlockSpec((2, tm, h), lambda i: (0, i, 0)),
            tok, tok,
            pl.BlockSpec((tm, h), lambda i: (i, 3)),
            tok,
            pl.BlockSpec((SUBLANES, h), lambda i: (jnp.maximum(i * per8 - 1, 0), 0)),
            pl.BlockSpec((SUBLANES, h), lambda i: (jnp.minimum((i + 1) * per8, nblk8 - 1), 0)),
            full2((1, h)), full2((1, h)), full2((SUBLANES, h)), full2((h, h)),
            pl.BlockSpec((None, 1, d), mod_map),
            full2((d, d)),
        ],
        out_specs=pl.BlockSpec((tm, d), lambda i: (i, 0)),
        out_shape=jax.ShapeDtypeStruct((n, d), F32),
        compiler_params=_cparams("parallel"),
        name="even_post",
    )(x, o2, bonus, g, y, z, z, z, ln_g.reshape(1, h), ln_b.reshape(1, h),
      jnp.pad(conv_w, ((0, SUBLANES - conv_w.shape[0]), (0, 0))), _head_ones(), gate, w_out.astype(BF16))


def _attn_kernel(lam_ref, q_ref, k_ref, v_ref, subln_ref, o_ref, *, out_scale):
    lam = lam_ref[0]
    hd2 = 2 * C_HD
    for h in range(C_HEADS):
        parts = []
        for j in range(2):
            lo = h * hd2 + j * C_HD
            s = _dot(q_ref[:, lo:lo + C_HD], k_ref[:, lo:lo + C_HD], NT)
            e = jnp.exp(s - jnp.max(s, axis=-1, keepdims=True))
            inv = 1.0 / jnp.sum(e, axis=-1, keepdims=True)
            parts.append(_dot(e.astype(BF16), v_ref[:, h * hd2:(h + 1) * hd2]) * inv)
        oh = parts[0] - lam * parts[1]
        y = oh * lax.rsqrt(jnp.mean(oh * oh, axis=-1, keepdims=True) + NORM_EPS)
        o_ref[:, h * hd2:(h + 1) * hd2] = y * subln_ref[...] * out_scale


def _diff_attention(lam, q, k, v, subln, lam_init, *, batch, tq_total, tq):
    d = q.shape[1]
    tk = k.shape[1]
    nq = tq_total // tq
    return pl.pallas_call(
        functools.partial(_attn_kernel, out_scale=1.0 - lam_init),
        grid=(batch, nq),
        in_specs=[
            pl.BlockSpec(memory_space=pltpu.SMEM),
            pl.BlockSpec((tq, d), lambda b, i: (b * nq + i, 0)),
            pl.BlockSpec((None, tk, d), lambda b, i: (b, 0, 0)),
            pl.BlockSpec((None, tk, d), lambda b, i: (b, 0, 0)),
            pl.BlockSpec((1, 2 * C_HD), lambda b, i: (0, 0)),
        ],
        out_specs=pl.BlockSpec((tq, d), lambda b, i: (b * nq + i, 0)),
        out_shape=jax.ShapeDtypeStruct((batch * tq_total, d), F32),
        compiler_params=_cparams("parallel", "parallel"),
        name="diff_attn",
    )(lam.reshape(1), q, k, v, subln.reshape(1, 2 * C_HD))


def _axial_rope_tables(n_tok):
    rows = n_tok // GRID_W
    row = jnp.repeat(jnp.arange(rows, dtype=F32), GRID_W)
    col = jnp.tile(jnp.arange(GRID_W, dtype=F32), rows)
    nf = C_HD // 4
    inv = ROPE_BASE ** (-jnp.arange(nf, dtype=F32) / nf)
    ar = row[:, None] * inv[None, :]
    ac = col[:, None] * inv[None, :]
    ang = jnp.concatenate([ar, ar, ac, ac], axis=-1)
    return jnp.cos(ang), jnp.sin(ang)


def _apply_rope(x, cos, sin):
    q = C_HD // 4
    first = (jnp.arange(x.shape[-1]) % (2 * q)) < q
    xr = jnp.where(first, -jnp.roll(x, -q, axis=-1), jnp.roll(x, q, axis=-1))
    return x * cos[None] + xr * sin[None]


N_TOP = P_TOPK + 1
CAND_PAIRS = [(i, j) for i in range(N_TOP) for j in range(N_TOP) if (i + 1) * (j + 1) <= N_TOP]
CAND_ROWS = -(-len(CAND_PAIRS) // SUBLANES) * SUBLANES
TOP_ROWS = -(-N_TOP // SUBLANES) * SUBLANES


def _take_top(x, n, emit, want_rank=False):
    rank = jnp.full(x.shape, float(n), F32) if want_rank else None
    for it in range(n):
        m = jnp.max(x, axis=0, keepdims=True)
        emit(it, m)
        hit = x == m
        if want_rank:
            rank = jnp.where(hit, float(it), rank)
        if it + 1 < n:
            x = jnp.where(hit, NEG_INF, x)
    return rank


def _pack_bf16_rows(x):
    return pltpu.bitcast(x.astype(BF16), jnp.int32)


def _peer_select_kernel(x_ref, g_ref, sc_ref, sh_ref, wq_ref, skh_ref, skl_ref,
                        hbt_out, l_out, c1_out, r2_out, e2_out, top_scr, cand_scr):
    x = x_ref[...]
    y = x * lax.rsqrt(jnp.mean(x * x, axis=-1, keepdims=True) + NORM_EPS)
    hmod = y * g_ref[...] * (1.0 + sc_ref[...]) + sh_ref[...]
    hbt_out[...] = hmod.T.astype(BF16)
    q = _dot(hmod.astype(BF16), wq_ref[...])
    half = P_QDIM // 2
    for h in range(P_HEADS):
        sT = []
        for j in range(2):
            hj = 2 * h + j
            qs = q[:, hj * half:(hj + 1) * half]
            q_hi, q_lo = _split2(qs)
            s = _dot(skh_ref[hj], q_hi, NT) + (_dot(skh_ref[hj], q_lo, NT) + _dot(skl_ref[hj], q_hi, NT))
            sT.append(s)

            def emit(it, m, j=j):
                top_scr[j, pl.ds(it, 1), :] = m
            rank = _take_top(s, N_TOP, emit, want_rank=(j == 1))

        cand_scr[...] = jnp.full(cand_scr.shape, NEG_INF, F32)
        for c, (i, j) in enumerate(CAND_PAIRS):
            cand_scr[pl.ds(c, 1), :] = top_scr[0, pl.ds(i, 1), :] + top_scr[1, pl.ds(j, 1), :]
        a1 = top_scr[0, pl.ds(0, 1), :]
        b1 = top_scr[1, pl.ds(0, 1), :]
        mx = a1 + b1
        st = {"z": None, "c16": None, "c17": None}

        def emit_c(it, m):
            if it < P_TOPK:
                e = jnp.exp(m - mx)
                st["z"] = e if st["z"] is None else st["z"] + e
            if it == P_TOPK - 1:
                st["c16"] = m
            if it == P_TOPK:
                st["c17"] = m
        _take_top(cand_scr[...], N_TOP, emit_c)
        tau = 0.5 * (st["c16"] + st["c17"])
        thr = tau - sT[0]
        cnt = jnp.zeros(thr.shape, F32)
        for k in range(N_TOP):
            cnt = cnt + jnp.where(top_scr[1, pl.ds(k, 1), :] >= thr, 1.0, 0.0)
        l_out[h] = cnt
        c1_out[h] = jnp.exp(sT[0] - a1) / st["z"]
        e2 = jnp.exp(sT[1] - b1)
        for g in range(e2.shape[1] // LANES):
            r2_out[h, g] = _pack_bf16_rows(rank[:, g * LANES:(g + 1) * LANES])
            e2_out[h, g] = _pack_bf16_rows(e2[:, g * LANES:(g + 1) * LANES])


def _peer_select(x, gnorm, scale, shift, w_q, sub_keys, *, tm, n_prompt, dec_seq):
    n, d = x.shape
    sk = sub_keys.reshape(2 * P_HEADS, P_NKEYS, P_QDIM // 2)
    skh, skl = _split2(sk)
    mod_map = lambda i: (_mod_row(i, tm, n_prompt, dec_seq), 0, 0)
    out = pl.BlockSpec((P_HEADS, P_NKEYS, tm), lambda i: (0, 0, i))
    sds = jax.ShapeDtypeStruct((P_HEADS, P_NKEYS, n), F32)
    out_g = pl.BlockSpec((P_HEADS, tm // LANES, P_NKEYS // 2, LANES), lambda i: (0, i, 0, 0))
    sds_g = jax.ShapeDtypeStruct((P_HEADS, n // LANES, P_NKEYS // 2, LANES), jnp.int32)
    return pl.pallas_call(
        _peer_select_kernel,
        grid=(n // tm,),
        in_specs=[pl.BlockSpec((tm, d), lambda i: (i, 0)),
                  pl.BlockSpec((1, d), lambda i: (0, 0)),
                  pl.BlockSpec((None, 1, d), mod_map),
                  pl.BlockSpec((None, 1, d), mod_map),
                  pl.BlockSpec(w_q.shape, lambda i: (0, 0)),
                  pl.BlockSpec(sk.shape, lambda i: (0, 0, 0)),
                  pl.BlockSpec(sk.shape, lambda i: (0, 0, 0))],
        out_specs=[pl.BlockSpec((d, tm), lambda i: (0, i)), out, out, out_g, out_g],
        out_shape=[jax.ShapeDtypeStruct((d, n), BF16), sds, sds, sds_g, sds_g],
        scratch_shapes=[pltpu.VMEM((2, TOP_ROWS, tm), F32), pltpu.VMEM((CAND_ROWS, tm), F32)],
        compiler_params=_cparams("parallel"),
        name="peer_select",
    )(x, gnorm.reshape(1, d), scale, shift, w_q.astype(BF16), skh, skl)


PAIR = 2
PAIR_ROWS = PAIR * P_NKEYS
I2_HALF = P_NKEYS // 2


def _gelu(x):
    return 0.5 * x * (1.0 + lax.erf(x * (1.0 / math.sqrt(2.0))))


def _peer_dense_kernel(hbt_ref, x_ref, gate_ref, ufirst_ref, unext_ref, vtprev_ref, vtlast_ref,
                       l_ref, c1_ref, r2_ref, e2_ref, o_ref, acc_ref, a_even, a_odd, act_even, act_odd,
                       *, stages, tm):
    c = pl.program_id(1)

    def put_act(act_ref, act):
        for q in range(stages):
            for lg in range(tm // LANES):
                act_ref[q, lg] = act[q * PAIR_ROWS:(q + 1) * PAIR_ROWS, lg * LANES:(lg + 1) * LANES]

    @pl.when(c == 0)
    def _():
        acc_ref[...] = jnp.zeros(acc_ref.shape, F32)
        a_odd[...] = jnp.zeros(a_odd.shape, BF16)
        put_act(act_even, _gelu(_dot(ufirst_ref[...], hbt_ref[...])))

    def chunk(act_cur, act_next, a_cur, a_prev):
        put_act(act_next, _gelu(_dot(unext_ref[...], hbt_ref[...])))
        for p in range(stages):
            for lg in range(tm // LANES):
                ls = slice(lg * LANES, (lg + 1) * LANES)
                for half in range(P_NKEYS // I2_HALF):
                    words = slice(half * I2_HALF // 2, (half + 1) * I2_HALF // 2)
                    w = [None] * PAIR
                    for h in range(P_HEADS):
                        r2 = pltpu.bitcast(r2_ref[h, lg, words, :], BF16)
                        e2 = pltpu.bitcast(e2_ref[h, lg, words, :], BF16)
                        for j in range(PAIR):
                            row = pl.ds(PAIR * p + j, 1)
                            cnt = jnp.broadcast_to(l_ref[h, row, ls], (I2_HALF, LANES)).astype(BF16)
                            c1 = jnp.broadcast_to(c1_ref[h, row, ls], (I2_HALF, LANES)).astype(BF16)
                            term = jnp.where(r2 < cnt, e2 * c1, jnp.zeros_like(e2))
                            w[j] = term if w[j] is None else w[j] + term
                    for j in range(PAIR):
                        off = j * P_NKEYS + half * I2_HALF
                        rs = slice(p * PAIR_ROWS + off, p * PAIR_ROWS + off + I2_HALF)
                        a_cur[rs, ls] = w[j] * act_cur[p, lg, off:off + I2_HALF, :].astype(BF16)
        acc_ref[...] += _dot(vtprev_ref[...], a_prev[...])

    @pl.when(c % 2 == 0)
    def _():
        chunk(act_even, act_odd, a_even, a_odd)

    @pl.when(c % 2 == 1)
    def _():
        chunk(act_odd, act_even, a_odd, a_even)

    @pl.when(c == pl.num_programs(1) - 1)
    def _():
        total = acc_ref[...] + _dot(vtlast_ref[...], a_odd[...])
        o_ref[...] = x_ref[...] + gate_ref[...] * total.T


def _peer_dense(x, hbt, gate, u, v, cnt, c1, r2, e2, *, tm, stages, n_prompt, dec_seq):
    n, d = x.shape
    ec = stages * PAIR_ROWS
    n_chunks = P_EXPERTS // ec
    assert n_chunks % 2 == 0
    u_bf = u.astype(BF16)
    vt_bf = v.astype(BF16).T
    mod_map = lambda i, c: (_mod_row(i, tm, n_prompt, dec_seq), 0, 0)
    small = pl.BlockSpec((P_HEADS, stages * PAIR, tm), lambda i, c: (0, c, i))
    big = pl.BlockSpec((P_HEADS, tm // LANES, P_NKEYS // 2, LANES), lambda i, c: (0, i, 0, 0))
    act_shape = (stages, tm // LANES, PAIR_ROWS, LANES)
    return pl.pallas_call(
        functools.partial(_peer_dense_kernel, stages=stages, tm=tm),
        grid=(n // tm, n_chunks),
        in_specs=[
            pl.BlockSpec((d, tm), lambda i, c: (0, i)),
            pl.BlockSpec((tm, d), lambda i, c: (i, 0)),
            pl.BlockSpec((None, 1, d), mod_map),
            pl.BlockSpec((ec, d), lambda i, c: (0, 0)),
            pl.BlockSpec((ec, d), lambda i, c: (jnp.minimum(c + 1, n_chunks - 1), 0)),
            pl.BlockSpec((d, ec), lambda i, c: (0, jnp.maximum(c - 1, 0))),
            pl.BlockSpec((d, ec), lambda i, c: (0, n_chunks - 1)),
            small, small, big, big,
        ],
        out_specs=pl.BlockSpec((tm, d), lambda i, c: (i, 0)),
        out_shape=jax.ShapeDtypeStruct((n, d), F32),
        scratch_shapes=[pltpu.VMEM((d, tm), F32), pltpu.VMEM((ec, tm), BF16), pltpu.VMEM((ec, tm), BF16),
                        pltpu.VMEM(act_shape, F32), pltpu.VMEM(act_shape, F32)],
        compiler_params=_cparams("parallel", "arbitrary"),
        name="peer_dense",
    )(hbt, x, gate, u_bf, u_bf, vt_bf, vt_bf, cnt, c1, r2, e2)


def _final_norm_kernel(x_ref, g_ref, o_ref):
    x = x_ref[...]
    o_ref[...] = x * lax.rsqrt(jnp.mean(x * x, axis=-1, keepdims=True) + NORM_EPS) * g_ref[...]


def _final_norm(x, g, *, tm):
    n, d = x.shape
    return pl.pallas_call(
        _final_norm_kernel,
        grid=(n // tm,),
        in_specs=[pl.BlockSpec((tm, d), lambda i: (i, 0)), pl.BlockSpec((1, d), lambda i: (0, 0))],
        out_specs=pl.BlockSpec((tm, d), lambda i: (i, 0)),
        out_shape=jax.ShapeDtypeStruct((n, d), F32),
        compiler_params=_cparams("parallel"),
        name="final_norm",
    )(x, g.reshape(1, d))


TOKEN_TILE = 256
DENSE_TILE = 512


def _tiles(n_prompt, seq, dec_seq):
    tm = math.gcd(math.gcd(seq, dec_seq), TOKEN_TILE)
    tm_dense = math.gcd(math.gcd(n_prompt, dec_seq), DENSE_TILE)
    return tm, tm_dense


def kernel(x_prompt, x_sample, c, state_rwkv, cache_k, cache_v, c_ctx, w_ada, b_ada, norm_mix, norm_ffn, w_in_even, rwkv_w0, rwkv_w_up, rwkv_a0, rwkv_a_up, rwkv_g_up, rwkv_k_k, rwkv_k_a, rwkv_r_k, rwkv_ln_g, rwkv_ln_b, conv_w, w_out_even, w_in_odd, diff_lam, diff_subln, w_out_odd, peer_w_q, peer_sub_keys, peer_u, peer_v, final_norm):
    bp, seq, d = x_prompt.shape
    bs, dec_seq, _ = x_sample.shape
    depth = w_ada.shape[0]
    n_prompt = bp * seq
    n_sample = bs * dec_seq
    tm, tm_dense = _tiles(n_prompt, seq, dec_seq)
    tok = dict(n_prompt=n_prompt, dec_seq=dec_seq)

    x = jnp.concatenate([x_prompt.reshape(n_prompt, d), x_sample.reshape(n_sample, d)], axis=0)

    rows = 1 + bs
    rows_pad = -(-rows // SUBLANES) * SUBLANES
    cond = jnp.zeros((rows_pad, d), F32).at[0].set(c_ctx).at[1:rows].set(c)
    mod = _adaln_all(cond, w_ada, b_ada)[:, :rows].reshape(depth, rows, 6, 1, d)
    mods = [[mod[l, :, m] for m in range(6)] for l in range(depth)]

    h = D_HALF
    perm = jnp.concatenate([jnp.arange(0, 3 * h), jnp.arange(3 * h + LORA_W, 6 * h + LORA_W),
                            jnp.arange(3 * h, 3 * h + LORA_W)])

    lg_p, vq_p = _chain_layout(bp)
    lg_s, vq_s = _chain_layout(bs)
    cos, sin = (jnp.tile(tab, (1, d // C_HD)) for tab in _axial_rope_tables(dec_seq))
    scale = C_HD ** -0.5

    new_s, new_k, new_v = [], [], []
    for l in range(depth):
        i = l // 2
        sh1, sc1, g1, sh2, sc2, g2 = mods[l]
        if l % 2 == 0:
            y = _norm_mod_matmul(x, norm_mix[l], sc1, sh1, w_in_even[i][:, perm], tm=tm, name="in_even", **tok)
            kk, w, kd, kka, bonus, g, z = _rwkv_prep(
                y, rwkv_w0[i], rwkv_w_up[i], rwkv_a0[i], rwkv_a_up[i], rwkv_g_up[i],
                rwkv_k_k[i], rwkv_k_a[i], rwkv_r_k[i].reshape(-1), tm=tm)
            r = y[:, 0:h]
            v = y[:, 2 * h:3 * h]
            outs = []
            for (lo, n_tok, batch, t, lg, vq, s0) in (
                    (0, n_prompt, bp, seq, lg_p, vq_p, None),
                    (n_prompt, n_sample, bs, dec_seq, lg_s, vq_s, state_rwkv[:, i])):
                sl = slice(lo, lo + n_tok)
                lk = lambda a: _to_lanes_k(a, batch, t, lg, vq)
                gsz = A_HEAD // vq
                s0l = (jnp.zeros((2, lg, A_HEAD, gsz, LANES), F32) if s0 is None
                       else _state_to_lanes(s0, lg, vq))
                o, sf = _wkv_scan(lk(r[sl]), lk(kk[sl]), _to_lanes_v(v[sl], batch, t, lg, vq),
                                  lk(w[:, sl]), lk(kd[:, sl]), lk(kka[:, sl]), s0l, tc=min(SCAN_CHUNK, t))
                outs.append(_from_lanes_v(o, batch, t, lg, vq))
                if s0 is None:
                    new_s.append(_state_from_lanes(sf, batch, lg, vq))
            o2 = jnp.concatenate(outs, axis=1)
            x = _even_post(x, o2, bonus, g, y, z, rwkv_ln_g[i], rwkv_ln_b[i], conv_w[i], g1, w_out_even[i],
                           tm=tm, n_prompt=n_prompt, seq=seq, dec_seq=dec_seq)
        else:
            lam_init = 0.8 - 0.6 * math.exp(-0.3 * l)
            lp = diff_lam[i].astype(F32)
            lam = jnp.exp(jnp.sum(lp[0] * lp[1])) - jnp.exp(jnp.sum(lp[2] * lp[3])) + lam_init
            y = _norm_mod_matmul(x, norm_mix[l], sc1, sh1, w_in_odd[i], tm=tm, name="in_odd", **tok)
            q, k, v = y[:, 0:d], y[:, d:2 * d], y[:, 2 * d:3 * d]
            new_k.append(k[:n_prompt].reshape(bp, seq, C_HEADS, 2, C_HD))
            new_v.append(v[:n_prompt].reshape(bp, seq, C_HEADS, 2 * C_HD))
            o_p = _diff_attention(lam, (q[:n_prompt] * scale).astype(BF16),
                                  k[:n_prompt].reshape(bp, seq, d).astype(BF16),
                                  v[:n_prompt].reshape(bp, seq, d).astype(BF16),
                                  diff_subln[i], lam_init, batch=bp, tq_total=seq, tq=min(seq, TOKEN_TILE))
            shp = (bs, dec_seq, d)
            qs = _apply_rope(q[n_prompt:].reshape(shp), cos, sin).reshape(n_sample, d)
            ks = _apply_rope(k[n_prompt:].reshape(shp), cos, sin)
            k_all = jnp.concatenate([ks, cache_k[:, i].reshape(bs, -1, d)], axis=1).astype(BF16)
            v_all = jnp.concatenate([v[n_prompt:].reshape(bs, dec_seq, d), cache_v[:, i].reshape(bs, -1, d)],
                                    axis=1).astype(BF16)
            o_s = _diff_attention(lam, (qs * scale).astype(BF16), k_all, v_all, diff_subln[i], lam_init,
                                  batch=bs, tq_total=dec_seq, tq=min(dec_seq, TOKEN_TILE))
            x = _proj_residual(x, jnp.concatenate([o_p, o_s], axis=0), g1, w_out_odd[i], tm=tm,
                               name="out_odd", **tok)

        hbt, cnt_, c1_, r2_, e2_ = _peer_select(x, norm_ffn[l], sc2, sh2, peer_w_q[l], peer_sub_keys[l],
                                                tm=tm, **tok)
        x = _peer_dense(x, hbt, g2, peer_u[l], peer_v[l], cnt_, c1_, r2_, e2_, tm=tm_dense, stages=4, **tok)

    yfin = _final_norm(x, final_norm, tm=tm)
    y_prompt = yfin[:n_prompt].reshape(bp, seq, d)
    y_sample = yfin[n_prompt:].reshape(bs, dec_seq, d)
    return (y_prompt, y_sample, jnp.stack(new_s, axis=1), jnp.stack(new_k, axis=1), jnp.stack(new_v, axis=1))
```

```python
import functools
import math

import jax
import jax.numpy as jnp
from jax import lax
from jax.experimental import pallas as pl
from jax.experimental.pallas import tpu as pltpu

D_MODEL = 1024
GRID_W = 64
ROPE_BASE = 10000.0
NORM_EPS = 1e-6
D_HALF = D_MODEL // 2
A_HEAD = 64
A_HEADS = D_HALF // A_HEAD
A_DECAY_LORA = 64
A_ICLR_LORA = 64
A_GATE_LORA = 128
GN_EPS = 64e-5
C_HEADS = 8
C_HD = D_MODEL // C_HEADS // 2
P_HEADS = 8
P_NKEYS = 128
P_EXPERTS = P_NKEYS * P_NKEYS
P_QDIM = 256
P_TOPK = 16
LORA_W = A_DECAY_LORA + A_ICLR_LORA + A_GATE_LORA
EVEN_IN = 6 * D_HALF + LORA_W

LANES = 128
SUBLANES = 8
VMEM_LIMIT = 56 * 1024 * 1024

F32 = jnp.float32
BF16 = jnp.bfloat16
NN = (((1,), (0,)), ((), ()))
NT = (((1,), (1,)), ((), ()))
NEG_INF = float("-inf")


def _cparams(*sem):
    return pltpu.CompilerParams(dimension_semantics=sem, vmem_limit_bytes=VMEM_LIMIT)


def _split2(a):
    hi = a.astype(BF16)
    lo = (a - hi.astype(F32)).astype(BF16)
    return hi, lo


def _split3(a):
    hi = a.astype(BF16)
    r = a - hi.astype(F32)
    mid = r.astype(BF16)
    lo = (r - mid.astype(F32)).astype(BF16)
    return hi, mid, lo


def _dot(a, b, dims=NN):
    return lax.dot_general(a, b, dims, preferred_element_type=F32)


def _dot3(a, b_hi, b_lo, dims=NN):
    a_hi, a_lo = _split2(a)
    return _dot(a_hi, b_hi, dims) + (_dot(a_hi, b_lo, dims) + _dot(a_lo, b_hi, dims))


def _dot_exact01(a, ones_bf16):
    hi, mid, lo = _split3(a)
    return _dot(hi, ones_bf16) + (_dot(mid, ones_bf16) + _dot(lo, ones_bf16))


def _mod_row(i, tm, n_prompt, dec_seq):
    np_tiles = n_prompt // tm
    per = dec_seq // tm
    return jnp.where(i < np_tiles, 0, 1 + (i - np_tiles) // per)


def _ada_kernel(c_ref, whi_ref, wlo_ref, b_ref, o_ref):
    c = c_ref[...]
    s = c * jax.nn.sigmoid(c)
    o_ref[0] = _dot3(s, whi_ref[0], wlo_ref[0]) + b_ref[0]


def _adaln_all(cond, w_ada, b_ada):
    depth, d, n6 = w_ada.shape
    rows = cond.shape[0]
    tn = n6 // 4
    whi, wlo = _split2(w_ada)
    return pl.pallas_call(
        _ada_kernel,
        grid=(depth, n6 // tn),
        in_specs=[
            pl.BlockSpec((rows, d), lambda l, j: (0, 0)),
            pl.BlockSpec((1, d, tn), lambda l, j: (l, 0, j)),
            pl.BlockSpec((1, d, tn), lambda l, j: (l, 0, j)),
            pl.BlockSpec((1, 1, tn), lambda l, j: (l, 0, j)),
        ],
        out_specs=pl.BlockSpec((1, rows, tn), lambda l, j: (l, 0, j)),
        out_shape=jax.ShapeDtypeStruct((depth, rows, n6), F32),
        compiler_params=_cparams("parallel", "parallel"),
        name="adaln",
    )(cond, whi, wlo, b_ada.reshape(depth, 1, n6))


def _nmm_kernel(x_ref, g_ref, sc_ref, sh_ref, w_ref, o_ref):
    x = x_ref[...]
    y = x * lax.rsqrt(jnp.mean(x * x, axis=-1, keepdims=True) + NORM_EPS)
    h = y * g_ref[...] * (1.0 + sc_ref[...]) + sh_ref[...]
    o_ref[...] = _dot(h.astype(BF16), w_ref[...])


def _norm_mod_matmul(x, gnorm, scale, shift, w, *, tm, n_prompt, dec_seq, name):
    n, d = x.shape
    nout = w.shape[1]
    tn = nout // 2
    mod_map = lambda j, i: (_mod_row(i, tm, n_prompt, dec_seq), 0, 0)
    return pl.pallas_call(
        _nmm_kernel,
        grid=(nout // tn, n // tm),
        in_specs=[
            pl.BlockSpec((tm, d), lambda j, i: (i, 0)),
            pl.BlockSpec((1, d), lambda j, i: (0, 0)),
            pl.BlockSpec((None, 1, d), mod_map),
            pl.BlockSpec((None, 1, d), mod_map),
            pl.BlockSpec((d, tn), lambda j, i: (0, j)),
        ],
        out_specs=pl.BlockSpec((tm, tn), lambda j, i: (i, j)),
        out_shape=jax.ShapeDtypeStruct((n, nout), F32),
        compiler_params=_cparams("arbitrary", "arbitrary"),
        name=name,
    )(x, gnorm.reshape(1, d), scale, shift, w.astype(BF16))


def _proj_res_kernel(x_ref, a_ref, gate_ref, w_ref, o_ref):
    o_ref[...] = x_ref[...] + gate_ref[...] * _dot(a_ref[...].astype(BF16), w_ref[...])


def _proj_residual(x, a, gate, w, *, tm, n_prompt, dec_seq, name):
    n, d = x.shape
    k = a.shape[1]
    mod_map = lambda i: (_mod_row(i, tm, n_prompt, dec_seq), 0, 0)
    return pl.pallas_call(
        _proj_res_kernel,
        grid=(n // tm,),
        in_specs=[
            pl.BlockSpec((tm, d), lambda i: (i, 0)),
            pl.BlockSpec((tm, k), lambda i: (i, 0)),
            pl.BlockSpec((None, 1, d), mod_map),
            pl.BlockSpec((k, d), lambda i: (0, 0)),
        ],
        out_specs=pl.BlockSpec((tm, d), lambda i: (i, 0)),
        out_shape=jax.ShapeDtypeStruct((n, d), F32),
        compiler_params=_cparams("parallel"),
        name=name,
    )(x, a, gate, w.astype(BF16))


def _rwkv_prep_kernel(r_ref, k_ref, v_ref, cg_ref, hc_ref, lora_ref,
                      w0_ref, wuph_ref, wupl_ref, a0_ref, auph_ref, aupl_ref, guph_ref, gupl_ref,
                      kk_ref_p, ka_ref_p, rk_ref_p, ones_ref,
                      r_out, v_out, kk_out, w_out, kd_out, kka_out, bonus_out, g_out, z_out):
    r = r_ref[...]
    k = k_ref[...]
    v = v_ref[...]
    r_out[...] = r
    v_out[...] = v
    lora = lora_ref[...]
    ones = ones_ref[...]
    wd = lora[:, :A_DECAY_LORA]
    ad = lora[:, A_DECAY_LORA:A_DECAY_LORA + A_ICLR_LORA]
    gd = lora[:, A_DECAY_LORA + A_ICLR_LORA:]

    kk = k * kk_ref_p[...]
    nrm = jnp.sqrt(_dot_exact01(kk * kk, ones))
    kk = kk / jnp.maximum(nrm, 1e-12)
    kk_out[...] = kk

    tw = jnp.tanh(wd)
    bonus = None
    for d in range(2):
        w_pre = w0_ref[d] + _dot3(tw, wuph_ref[d], wupl_ref[d])
        w_out[d] = jnp.exp(-(math.exp(-0.5) * jax.nn.sigmoid(w_pre)))
        a = jax.nn.sigmoid(a0_ref[d] + _dot3(ad, auph_ref[d], aupl_ref[d]))
        kd = k * (1.0 + (a - 1.0) * ka_ref_p[...])
        kd_out[d] = kd
        kka_out[d] = kk * a
        b = _dot_exact01(r * kd * rk_ref_p[...], ones) * v
        bonus = b if bonus is None else bonus + b
    bonus_out[...] = bonus
    g_out[...] = _dot3(jax.nn.sigmoid(gd), guph_ref[...], gupl_ref[...])
    z_out[...] = cg_ref[...] * hc_ref[...]


def _head_ones():
    idx = jnp.arange(D_HALF) // A_HEAD
    return (idx[:, None] == idx[None, :]).astype(BF16)


def _rwkv_prep(y, w0, w_up, a0, a_up, g_up, k_k, k_a, r_k, *, tm, row0, n):
    off = row0 // tm
    h = D_HALF
    wuph, wupl = _split2(w_up)
    auph, aupl = _split2(a_up)
    guph, gupl = _split2(g_up)
    col = lambda c: pl.BlockSpec((tm, h), lambda i: (i + off, c))
    full2 = lambda s: pl.BlockSpec(s, lambda i: (0, 0))
    full3 = lambda s: pl.BlockSpec(s, lambda i: (0, 0, 0))
    tok = pl.BlockSpec((tm, h), lambda i: (i, 0))
    tok2 = pl.BlockSpec((2, tm, h), lambda i: (0, i, 0))
    sds = lambda *s: jax.ShapeDtypeStruct(s, F32)
    return pl.pallas_call(
        _rwkv_prep_kernel,
        grid=(n // tm,),
        in_specs=[col(0), col(1), col(2), col(4), col(5),
                  pl.BlockSpec((tm, LORA_W), lambda i: (i + off, 6 * h // LORA_W)),
                  full3((2, 1, h)), full3((2, A_DECAY_LORA, h)), full3((2, A_DECAY_LORA, h)),
                  full3((2, 1, h)), full3((2, A_ICLR_LORA, h)), full3((2, A_ICLR_LORA, h)),
                  full2((A_GATE_LORA, h)), full2((A_GATE_LORA, h)),
                  full2((1, h)), full2((1, h)), full2((1, h)), full2((h, h))],
        out_specs=[tok, tok, tok, tok2, tok2, tok2, tok, tok, tok],
        out_shape=[sds(n, h), sds(n, h), sds(n, h), sds(2, n, h), sds(2, n, h), sds(2, n, h),
                   sds(n, h), sds(n, h), sds(n, h)],
        compiler_params=_cparams("parallel"),
        name="rwkv_prep",
    )(y, y, y, y, y, y,
      w0.reshape(2, 1, h), wuph, wupl, a0.reshape(2, 1, h), auph, aupl, guph, gupl,
      k_k.reshape(1, h), k_a.reshape(1, h), r_k.reshape(1, h), _head_ones())


SCAN_ROW_BLOCK = 32
SCAN_CHUNK = 16


def _scan_kernel(r_ref, kk_ref, v_ref, w_ref, kd_ref, kka_ref, s0_ref, o_ref, sf_ref, s_scr, *, tc, g):
    d = pl.program_id(0)
    c = pl.program_id(2)
    nk = A_HEAD
    parts = 4

    @pl.when(c == 0)
    def _():
        s_scr[...] = s0_ref[...]

    gb = min(g, SCAN_ROW_BLOCK)

    def row(ref, t, k):
        return jnp.broadcast_to(ref[t, pl.ds(k, 1), :], (gb, LANES))

    def tree(xs):
        while len(xs) > 1:
            xs = [xs[i] + xs[i + 1] for i in range(0, len(xs) - 1, 2)] + ([xs[-1]] if len(xs) % 2 else [])
        return xs[0]

    def step(i, carry):
        t = i + d * (tc - 1 - 2 * i)
        for g0 in range(0, g, gb):
            rows = slice(g0, g0 + gb)
            acc = [None] * parts
            for k in range(nk):
                p = s_scr[k, rows, :] * row(kk_ref, t, k)
                acc[k % parts] = p if acc[k % parts] is None else acc[k % parts] + p
            sa = tree(acc)
            vv = v_ref[t, rows, :]
            acc = [None] * parts
            for k in range(nk):
                sn = s_scr[k, rows, :] * row(w_ref, t, k) - sa * row(kka_ref, t, k) + vv * row(kd_ref, t, k)
                s_scr[k, rows, :] = sn
                p = sn * row(r_ref, t, k)
                acc[k % parts] = p if acc[k % parts] is None else acc[k % parts] + p
            o_ref[t, rows, :] = tree(acc)
        return carry

    lax.fori_loop(0, tc, step, 0)

    @pl.when(c == pl.num_programs(2) - 1)
    def _():
        sf_ref[...] = s_scr[...]


def _wkv_scan(r, kk, v, w, kd, kka, s0, *, tc):
    lg, t, nk, _ = r.shape
    g = v.shape[2]
    nt = t // tc
    tmap = lambda d, l, c: c + d * (nt - 1 - 2 * c)
    shared = lambda rows: pl.BlockSpec((None, tc, rows, LANES), lambda d, l, c: (l, tmap(d, l, c), 0, 0))
    perdir = pl.BlockSpec((None, None, tc, nk, LANES), lambda d, l, c: (d, l, tmap(d, l, c), 0, 0))
    state = pl.BlockSpec((None, None, nk, g, LANES), lambda d, l, c: (d, l, 0, 0, 0))
    return pl.pallas_call(
        functools.partial(_scan_kernel, tc=tc, g=g),
        grid=(2, lg, nt),
        in_specs=[shared(nk), shared(nk), shared(g), perdir, perdir, perdir, state],
        out_specs=[pl.BlockSpec((None, None, tc, g, LANES), lambda d, l, c: (d, l, tmap(d, l, c), 0, 0)), state],
        out_shape=[jax.ShapeDtypeStruct((2, lg, t, g, LANES), F32),
                   jax.ShapeDtypeStruct((2, lg, nk, g, LANES), F32)],
        scratch_shapes=[pltpu.VMEM((nk, g, LANES), F32)],
        compiler_params=_cparams("parallel", "parallel", "arbitrary"),
        name="wkv_scan",
    )(r, kk, v, w, kd, kka, s0)


def _chain_layout(batch):
    chains = batch * A_HEADS
    if chains >= LANES:
        assert chains % LANES == 0
        return chains // LANES, 1
    assert LANES % chains == 0 and A_HEAD % (LANES // chains) == 0
    return 1, LANES // chains


def _to_lanes_k(x, batch, t, lg, vq):
    lead = x.shape[:-2]
    nl = len(lead)
    chains = (batch // lg) * A_HEADS
    x = x.reshape(lead + (lg, 1, batch // lg, t, A_HEADS, A_HEAD))
    x = jnp.broadcast_to(x, lead + (lg, vq, batch // lg, t, A_HEADS, A_HEAD))
    perm = tuple(range(nl)) + tuple(nl + p for p in (0, 3, 5, 1, 2, 4))
    return jnp.transpose(x, perm).reshape(lead + (lg, t, A_HEAD, vq * chains))


def _to_lanes_v(x, batch, t, lg, vq):
    gsz = A_HEAD // vq
    x = x.reshape(lg, batch // lg, t, A_HEADS, vq, gsz)
    x = jnp.transpose(x, (0, 2, 5, 4, 1, 3))
    return x.reshape(lg, t, gsz, vq * (batch // lg) * A_HEADS)


def _from_lanes_v(o, batch, t, lg, vq):
    lead = o.shape[:-4]
    gsz = A_HEAD // vq
    o = o.reshape(lead + (lg, t, gsz, vq, batch // lg, A_HEADS))
    nl = len(lead)
    perm = tuple(range(nl)) + tuple(nl + p for p in (0, 4, 1, 5, 3, 2))
    return jnp.transpose(o, perm).reshape(lead + (batch * t, D_HALF))


def _state_to_lanes(s, lg, vq):
    batch = s.shape[0]
    gsz = A_HEAD // vq
    s = s.reshape(lg, batch // lg, 2, A_HEADS, vq, gsz, A_HEAD)
    s = jnp.transpose(s, (2, 0, 6, 5, 4, 1, 3))
    return s.reshape(2, lg, A_HEAD, gsz, LANES)


def _state_from_lanes(s, batch, lg, vq):
    gsz = A_HEAD // vq
    s = s.reshape(2, lg, A_HEAD, gsz, vq, batch // lg, A_HEADS)
    s = jnp.transpose(s, (1, 5, 0, 6, 4, 3, 2))
    return s.reshape(batch, 2, A_HEADS, A_HEAD, A_HEAD)


def _even_post_kernel(x_ref, o_ref_in, bonus_ref, g_ref, bg_ref, z_ref, zprev_ref, znext_ref,
                      lng_ref, lnb_ref, cw_ref, ones_ref, gate_ref, w_ref, out_ref,
                      *, tm, np_tiles, seq_tiles, dec_tiles):
    i = pl.program_id(0)
    ones = ones_ref[...]
    inv = 1.0 / A_HEAD
    o = o_ref_in[0] + o_ref_in[1]
    mu = _dot_exact01(o, ones) * inv
    xc = o - mu
    var = _dot_exact01(xc * xc, ones) * inv
    on = xc * lax.rsqrt(var + GN_EPS) * lng_ref[...] + lnb_ref[...]
    ya = (on + bonus_ref[...]) * g_ref[...]

    pos = jnp.where(i < np_tiles, i % seq_tiles, (i - np_tiles) % dec_tiles)
    last = jnp.where(i < np_tiles, seq_tiles - 1, dec_tiles - 1)
    z = z_ref[...]
    rows = lax.broadcasted_iota(jnp.int32, z.shape, 0)
    prev_row = jnp.where(pos == 0, 0.0, zprev_ref[pl.ds(SUBLANES - 1, 1), :])
    next_row = jnp.where(pos == last, 0.0, znext_ref[pl.ds(0, 1), :])
    zp = jnp.where(rows == 0, prev_row, pltpu.roll(z, 1, axis=0))
    zn = jnp.where(rows == tm - 1, next_row, pltpu.roll(z, tm - 1, axis=0))
    yb = bg_ref[...] * (zp * cw_ref[pl.ds(0, 1), :] + z * cw_ref[pl.ds(1, 1), :] + zn * cw_ref[pl.ds(2, 1), :])

    h = D_HALF
    y = _dot(ya.astype(BF16), w_ref[pl.ds(0, h), :]) + _dot(yb.astype(BF16), w_ref[pl.ds(h, h), :])
    out_ref[...] = x_ref[...] + gate_ref[...] * y


def _even_post(x, o2, bonus, g, y, z, ln_g, ln_b, conv_w, gate, w_out, *, tm, n_prompt, seq, dec_seq):
    n, d = x.shape
    h = D_HALF
    nblk8 = n // SUBLANES
    per8 = tm // SUBLANES
    tok = pl.BlockSpec((tm, h), lambda i: (i, 0))
    full2 = lambda s: pl.BlockSpec(s, lambda i: (0, 0))
    mod_map = lambda i: (_mod_row(i, tm, n_prompt, dec_seq), 0, 0)
    return pl.pallas_call(
        functools.partial(_even_post_kernel, tm=tm, np_tiles=n_prompt // tm, seq_tiles=seq // tm,
                          dec_tiles=dec_seq // tm),
        grid=(n // tm,),
        in_specs=[
            pl.BlockSpec((tm, d), lambda i: (i, 0)),
            pl.BlockSpec((2, tm, h), lambda i: (0, i, 0)),
            tok, tok,
            pl.BlockSpec((tm, h), lambda i: (i, 3)),
            tok,
            pl.BlockSpec((SUBLANES, h), lambda i: (jnp.maximum(i * per8 - 1, 0), 0)),
            pl.BlockSpec((SUBLANES, h), lambda i: (jnp.minimum((i + 1) * per8, nblk8 - 1), 0)),
            full2((1, h)), full2((1, h)), full2((SUBLANES, h)), full2((h, h)),
            pl.BlockSpec((None, 1, d), mod_map),
            full2((d, d)),
        ],
        out_specs=pl.BlockSpec((tm, d), lambda i: (i, 0)),
        out_shape=jax.ShapeDtypeStruct((n, d), F32),
        compiler_params=_cparams("parallel"),
        name="even_post",
    )(x, o2, bonus, g, y, z, z, z, ln_g.reshape(1, h), ln_b.reshape(1, h),
      jnp.pad(conv_w, ((0, SUBLANES - conv_w.shape[0]), (0, 0))), _head_ones(), gate, w_out.astype(BF16))


def _attn_kernel(lam_ref, q_ref, k_ref, v_ref, subln_ref, o_ref, *, out_scale):
    lam = lam_ref[0]
    hd2 = 2 * C_HD
    for h in range(C_HEADS):
        parts = []
        for j in range(2):
            lo = h * hd2 + j * C_HD
            s = _dot(q_ref[:, lo:lo + C_HD], k_ref[:, lo:lo + C_HD], NT)
            e = jnp.exp(s - jnp.max(s, axis=-1, keepdims=True))
            inv = 1.0 / jnp.sum(e, axis=-1, keepdims=True)
            parts.append(_dot(e.astype(BF16), v_ref[:, h * hd2:(h + 1) * hd2]) * inv)
        oh = parts[0] - lam * parts[1]
        y = oh * lax.rsqrt(jnp.mean(oh * oh, axis=-1, keepdims=True) + NORM_EPS)
        o_ref[:, h * hd2:(h + 1) * hd2] = y * subln_ref[...] * out_scale


def _diff_attention(lam, q, k, v, subln, lam_init, *, batch, tq_total, tq):
    d = q.shape[1]
    tk = k.shape[1]
    nq = tq_total // tq
    return pl.pallas_call(
        functools.partial(_attn_kernel, out_scale=1.0 - lam_init),
        grid=(batch, nq),
        in_specs=[
            pl.BlockSpec(memory_space=pltpu.SMEM),
            pl.BlockSpec((tq, d), lambda b, i: (b * nq + i, 0)),
            pl.BlockSpec((None, tk, d), lambda b, i: (b, 0, 0)),
            pl.BlockSpec((None, tk, d), lambda b, i: (b, 0, 0)),
            pl.BlockSpec((1, 2 * C_HD), lambda b, i: (0, 0)),
        ],
        out_specs=pl.BlockSpec((tq, d), lambda b, i: (b * nq + i, 0)),
        out_shape=jax.ShapeDtypeStruct((batch * tq_total, d), F32),
        compiler_params=_cparams("parallel", "parallel"),
        name="diff_attn",
    )(lam.reshape(1), q, k, v, subln.reshape(1, 2 * C_HD))


def _axial_rope_tables(n_tok):
    rows = n_tok // GRID_W
    row = jnp.repeat(jnp.arange(rows, dtype=F32), GRID_W)
    col = jnp.tile(jnp.arange(GRID_W, dtype=F32), rows)
    nf = C_HD // 4
    inv = ROPE_BASE ** (-jnp.arange(nf, dtype=F32) / nf)
    ar = row[:, None] * inv[None, :]
    ac = col[:, None] * inv[None, :]
    ang = jnp.concatenate([ar, ar, ac, ac], axis=-1)
    return jnp.cos(ang), jnp.sin(ang)


def _apply_rope(x, cos, sin):
    q = C_HD // 4
    first = (jnp.arange(x.shape[-1]) % (2 * q)) < q
    xr = jnp.where(first, -jnp.roll(x, -q, axis=-1), jnp.roll(x, q, axis=-1))
    return x * cos[None] + xr * sin[None]


N_TOP = P_TOPK + 1
CAND_PAIRS = [(i, j) for i in range(N_TOP) for j in range(N_TOP) if (i + 1) * (j + 1) <= N_TOP]
CAND_ROWS = -(-len(CAND_PAIRS) // SUBLANES) * SUBLANES
TOP_ROWS = -(-N_TOP // SUBLANES) * SUBLANES


def _take_top(x, n, emit, want_rank=False):
    rank = jnp.full(x.shape, float(n), F32) if want_rank else None
    for it in range(n):
        m = jnp.max(x, axis=0, keepdims=True)
        emit(it, m)
        hit = x == m
        if want_rank:
            rank = jnp.where(hit, float(it), rank)
        if it + 1 < n:
            x = jnp.where(hit, NEG_INF, x)
    return rank


def _pack_bf16_rows(x):
    return pltpu.bitcast(x.astype(BF16), jnp.int32)


def _peer_select_kernel(x_ref, g_ref, sc_ref, sh_ref, wq_ref, skh_ref, skl_ref,
                        hbt_out, l_out, c1_out, r2_out, e2_out, top_scr, cand_scr):
    x = x_ref[...]
    y = x * lax.rsqrt(jnp.mean(x * x, axis=-1, keepdims=True) + NORM_EPS)
    hmod = y * g_ref[...] * (1.0 + sc_ref[...]) + sh_ref[...]
    hbt_out[...] = hmod.T.astype(BF16)
    q = _dot(hmod.astype(BF16), wq_ref[...])
    half = P_QDIM // 2
    for h in range(P_HEADS):
        sT = []
        for j in range(2):
            hj = 2 * h + j
            qs = q[:, hj * half:(hj + 1) * half]
            q_hi, q_lo = _split2(qs)
            s = _dot(skh_ref[hj], q_hi, NT) + (_dot(skh_ref[hj], q_lo, NT) + _dot(skl_ref[hj], q_hi, NT))
            sT.append(s)

            def emit(it, m, j=j):
                top_scr[j, pl.ds(it, 1), :] = m
            rank = _take_top(s, N_TOP, emit, want_rank=(j == 1))

        cand_scr[...] = jnp.full(cand_scr.shape, NEG_INF, F32)
        for c, (i, j) in enumerate(CAND_PAIRS):
            cand_scr[pl.ds(c, 1), :] = top_scr[0, pl.ds(i, 1), :] + top_scr[1, pl.ds(j, 1), :]
        a1 = top_scr[0, pl.ds(0, 1), :]
        b1 = top_scr[1, pl.ds(0, 1), :]
        mx = a1 + b1
        st = {"z": None, "c16": None, "c17": None}

        def emit_c(it, m):
            if it < P_TOPK:
                e = jnp.exp(m - mx)
                st["z"] = e if st["z"] is None else st["z"] + e
            if it == P_TOPK - 1:
                st["c16"] = m
            if it == P_TOPK:
                st["c17"] = m
        _take_top(cand_scr[...], N_TOP, emit_c)
        tau = 0.5 * (st["c16"] + st["c17"])
        thr = tau - sT[0]
        cnt = jnp.zeros(thr.shape, F32)
        for k in range(N_TOP):
            cnt = cnt + jnp.where(top_scr[1, pl.ds(k, 1), :] >= thr, 1.0, 0.0)
        l_out[h] = cnt
        c1_out[h] = jnp.exp(sT[0] - a1) / st["z"]
        e2 = jnp.exp(sT[1] - b1)
        for g in range(e2.shape[1] // LANES):
            r2_out[h, g] = _pack_bf16_rows(rank[:, g * LANES:(g + 1) * LANES])
            e2_out[h, g] = _pack_bf16_rows(e2[:, g * LANES:(g + 1) * LANES])


def _peer_select(x, gnorm, scale, shift, w_q, sub_keys, *, tm, n_prompt, dec_seq):
    n, d = x.shape
    sk = sub_keys.reshape(2 * P_HEADS, P_NKEYS, P_QDIM // 2)
    skh, skl = _split2(sk)
    mod_map = lambda i: (_mod_row(i, tm, n_prompt, dec_seq), 0, 0)
    out = pl.BlockSpec((P_HEADS, P_NKEYS, tm), lambda i: (0, 0, i))
    sds = jax.ShapeDtypeStruct((P_HEADS, P_NKEYS, n), F32)
    out_g = pl.BlockSpec((P_HEADS, tm // LANES, P_NKEYS // 2, LANES), lambda i: (0, i, 0, 0))
    sds_g = jax.ShapeDtypeStruct((P_HEADS, n // LANES, P_NKEYS // 2, LANES), jnp.int32)
    return pl.pallas_call(
        _peer_select_kernel,
        grid=(n // tm,),
        in_specs=[pl.BlockSpec((tm, d), lambda i: (i, 0)),
                  pl.BlockSpec((1, d), lambda i: (0, 0)),
                  pl.BlockSpec((None, 1, d), mod_map),
                  pl.BlockSpec((None, 1, d), mod_map),
                  pl.BlockSpec(w_q.shape, lambda i: (0, 0)),
                  pl.BlockSpec(sk.shape, lambda i: (0, 0, 0)),
                  pl.BlockSpec(sk.shape, lambda i: (0, 0, 0))],
        out_specs=[pl.BlockSpec((d, tm), lambda i: (0, i)), out, out, out_g, out_g],
        out_shape=[jax.ShapeDtypeStruct((d, n), BF16), sds, sds, sds_g, sds_g],
        scratch_shapes=[pltpu.VMEM((2, TOP_ROWS, tm), F32), pltpu.VMEM((CAND_ROWS, tm), F32)],
        compiler_params=_cparams("parallel"),
        name="peer_select",
    )(x, gnorm.reshape(1, d), scale, shift, w_q.astype(BF16), skh, skl)


PAIR = 2
PAIR_ROWS = PAIR * P_NKEYS
I2_HALF = P_NKEYS // 2


def _gelu(x):
    return 0.5 * x * (1.0 + lax.erf(x * (1.0 / math.sqrt(2.0))))


def _peer_dense_kernel(hbt_ref, x_ref, gate_ref, ufirst_ref, unext_ref, vtprev_ref, vtlast_ref,
                       l_ref, c1_ref, r2_ref, e2_ref, o_ref, acc_ref, a_even, a_odd, act_even, act_odd,
                       *, stages, tm):
    c = pl.program_id(1)

    def put_act(act_ref, act):
        for q in range(stages):
            for lg in range(tm // LANES):
                act_ref[q, lg] = act[q * PAIR_ROWS:(q + 1) * PAIR_ROWS, lg * LANES:(lg + 1) * LANES]

    @pl.when(c == 0)
    def _():
        acc_ref[...] = jnp.zeros(acc_ref.shape, F32)
        a_odd[...] = jnp.zeros(a_odd.shape, BF16)
        put_act(act_even, _gelu(_dot(ufirst_ref[...], hbt_ref[...])))

    def chunk(act_cur, act_next, a_cur, a_prev):
        put_act(act_next, _gelu(_dot(unext_ref[...], hbt_ref[...])))
        for p in range(stages):
            for lg in range(tm // LANES):
                ls = slice(lg * LANES, (lg + 1) * LANES)
                for half in range(P_NKEYS // I2_HALF):
                    words = slice(half * I2_HALF // 2, (half + 1) * I2_HALF // 2)
                    w = [None] * PAIR
                    for h in range(P_HEADS):
                        r2 = pltpu.bitcast(r2_ref[h, lg, words, :], BF16)
                        e2 = pltpu.bitcast(e2_ref[h, lg, words, :], BF16)
                        for j in range(PAIR):
                            row = pl.ds(PAIR * p + j, 1)
                            cnt = jnp.broadcast_to(l_ref[h, row, ls], (I2_HALF, LANES)).astype(BF16)
                            c1 = jnp.broadcast_to(c1_ref[h, row, ls], (I2_HALF, LANES)).astype(BF16)
                            term = jnp.where(r2 < cnt, e2 * c1, jnp.zeros_like(e2))
                            w[j] = term if w[j] is None else w[j] + term
                    for j in range(PAIR):
                        off = j * P_NKEYS + half * I2_HALF
                        rs = slice(p * PAIR_ROWS + off, p * PAIR_ROWS + off + I2_HALF)
                        a_cur[rs, ls] = w[j] * act_cur[p, lg, off:off + I2_HALF, :].astype(BF16)
        acc_ref[...] += _dot(vtprev_ref[...], a_prev[...])

    @pl.when(c % 2 == 0)
    def _():
        chunk(act_even, act_odd, a_even, a_odd)

    @pl.when(c % 2 == 1)
    def _():
        chunk(act_odd, act_even, a_odd, a_even)

    @pl.when(c == pl.num_programs(1) - 1)
    def _():
        total = acc_ref[...] + _dot(vtlast_ref[...], a_odd[...])
        o_ref[...] = x_ref[...] + gate_ref[...] * total.T


def _peer_dense(x, hbt, gate, u, v, cnt, c1, r2, e2, *, tm, stages, n_prompt, dec_seq):
    n, d = x.shape
    ec = stages * PAIR_ROWS
    n_chunks = P_EXPERTS // ec
    assert n_chunks % 2 == 0
    u_bf = u.astype(BF16)
    vt_bf = v.astype(BF16).T
    mod_map = lambda i, c: (_mod_row(i, tm, n_prompt, dec_seq), 0, 0)
    small = pl.BlockSpec((P_HEADS, stages * PAIR, tm), lambda i, c: (0, c, i))
    big = pl.BlockSpec((P_HEADS, tm // LANES, P_NKEYS // 2, LANES), lambda i, c: (0, i, 0, 0))
    act_shape = (stages, tm // LANES, PAIR_ROWS, LANES)
    return pl.pallas_call(
        functools.partial(_peer_dense_kernel, stages=stages, tm=tm),
        grid=(n // tm, n_chunks),
        in_specs=[
            pl.BlockSpec((d, tm), lambda i, c: (0, i)),
            pl.BlockSpec((tm, d), lambda i, c: (i, 0)),
            pl.BlockSpec((None, 1, d), mod_map),
            pl.BlockSpec((ec, d), lambda i, c: (0, 0)),
            pl.BlockSpec((ec, d), lambda i, c: (jnp.minimum(c + 1, n_chunks - 1), 0)),
            pl.BlockSpec((d, ec), lambda i, c: (0, jnp.maximum(c - 1, 0))),
            pl.BlockSpec((d, ec), lambda i, c: (0, n_chunks - 1)),
            small, small, big, big,
        ],
        out_specs=pl.BlockSpec((tm, d), lambda i, c: (i, 0)),
        out_shape=jax.ShapeDtypeStruct((n, d), F32),
        scratch_shapes=[pltpu.VMEM((d, tm), F32), pltpu.VMEM((ec, tm), BF16), pltpu.VMEM((ec, tm), BF16),
                        pltpu.VMEM(act_shape, F32), pltpu.VMEM(act_shape, F32)],
        compiler_params=_cparams("parallel", "arbitrary"),
        name="peer_dense",
    )(hbt, x, gate, u_bf, u_bf, vt_bf, vt_bf, cnt, c1, r2, e2)


def _final_norm_kernel(x_ref, g_ref, o_ref):
    x = x_ref[...]
    o_ref[...] = x * lax.rsqrt(jnp.mean(x * x, axis=-1, keepdims=True) + NORM_EPS) * g_ref[...]


def _final_norm(x, g, *, tm):
    n, d = x.shape
    return pl.pallas_call(
        _final_norm_kernel,
        grid=(n // tm,),
        in_specs=[pl.BlockSpec((tm, d), lambda i: (i, 0)), pl.BlockSpec((1, d), lambda i: (0, 0))],
        out_specs=pl.BlockSpec((tm, d), lambda i: (i, 0)),
        out_shape=jax.ShapeDtypeStruct((n, d), F32),
        compiler_params=_cparams("parallel"),
        name="final_norm",
    )(x, g.reshape(1, d))


TOKEN_TILE = 256
DENSE_TILE = 512


def _tiles(n_prompt, seq, dec_seq):
    tm = math.gcd(math.gcd(seq, dec_seq), TOKEN_TILE)
    tm_dense = math.gcd(math.gcd(n_prompt, dec_seq), DENSE_TILE)
    return tm, tm_dense


def kernel(x_prompt, x_sample, c, state_rwkv, cache_k, cache_v, c_ctx, w_ada, b_ada, norm_mix, norm_ffn, w_in_even, rwkv_w0, rwkv_w_up, rwkv_a0, rwkv_a_up, rwkv_g_up, rwkv_k_k, rwkv_k_a, rwkv_r_k, rwkv_ln_g, rwkv_ln_b, conv_w, w_out_even, w_in_odd, diff_lam, diff_subln, w_out_odd, peer_w_q, peer_sub_keys, peer_u, peer_v, final_norm):
    bp, seq, d = x_prompt.shape
    bs, dec_seq, _ = x_sample.shape
    depth = w_ada.shape[0]
    n_prompt = bp * seq
    n_sample = bs * dec_seq
    tm, tm_dense = _tiles(n_prompt, seq, dec_seq)
    tok = dict(n_prompt=n_prompt, dec_seq=dec_seq)

    x = jnp.concatenate([x_prompt.reshape(n_prompt, d), x_sample.reshape(n_sample, d)], axis=0)

    rows = 1 + bs
    rows_pad = -(-rows // SUBLANES) * SUBLANES
    cond = jnp.zeros((rows_pad, d), F32).at[0].set(c_ctx).at[1:rows].set(c)
    mod = _adaln_all(cond, w_ada, b_ada)[:, :rows].reshape(depth, rows, 6, 1, d)
    mods = [[mod[l, :, m] for m in range(6)] for l in range(depth)]

    h = D_HALF
    perm = jnp.concatenate([jnp.arange(0, 3 * h), jnp.arange(3 * h + LORA_W, 6 * h + LORA_W),
                            jnp.arange(3 * h, 3 * h + LORA_W)])

    lg_p, vq_p = _chain_layout(bp)
    lg_s, vq_s = _chain_layout(bs)
    cos, sin = (jnp.tile(tab, (1, d // C_HD)) for tab in _axial_rope_tables(dec_seq))
    scale = C_HD ** -0.5

    new_s, new_k, new_v = [], [], []
    for l in range(depth):
        i = l // 2
        sh1, sc1, g1, sh2, sc2, g2 = mods[l]
        if l % 2 == 0:
            y = _norm_mod_matmul(x, norm_mix[l], sc1, sh1, w_in_even[i][:, perm], tm=tm, name="in_even", **tok)
            outs, tails = [], []
            for (lo, n_tok, batch, t, lg, vq, s0) in (
                    (0, n_prompt, bp, seq, lg_p, vq_p, None),
                    (n_prompt, n_sample, bs, dec_seq, lg_s, vq_s, state_rwkv[:, i])):
                r, v, kk, w, kd, kka, bonus, g, z = _rwkv_prep(
                    y, rwkv_w0[i], rwkv_w_up[i], rwkv_a0[i], rwkv_a_up[i], rwkv_g_up[i],
                    rwkv_k_k[i], rwkv_k_a[i], rwkv_r_k[i].reshape(-1), tm=tm, row0=lo, n=n_tok)
                tails.append((bonus, g, z))
                lk = lambda a: _to_lanes_k(a, batch, t, lg, vq)
                gsz = A_HEAD // vq
                s0l = (jnp.zeros((2, lg, A_HEAD, gsz, LANES), F32) if s0 is None
                       else _state_to_lanes(s0, lg, vq))
                o, sf = _wkv_scan(lk(r), lk(kk), _to_lanes_v(v, batch, t, lg, vq),
                                  lk(w), lk(kd), lk(kka), s0l, tc=min(SCAN_CHUNK, t))
                outs.append(_from_lanes_v(o, batch, t, lg, vq))
                if s0 is None:
                    new_s.append(_state_from_lanes(sf, batch, lg, vq))
            o2 = jnp.concatenate(outs, axis=1)
            bonus, g, z = (jnp.concatenate(parts, axis=0) for parts in zip(*tails))
            x = _even_post(x, o2, bonus, g, y, z, rwkv_ln_g[i], rwkv_ln_b[i], conv_w[i], g1, w_out_even[i],
                           tm=tm, n_prompt=n_prompt, seq=seq, dec_seq=dec_seq)
        else:
            lam_init = 0.8 - 0.6 * math.exp(-0.3 * l)
            lp = diff_lam[i].astype(F32)
            lam = jnp.exp(jnp.sum(lp[0] * lp[1])) - jnp.exp(jnp.sum(lp[2] * lp[3])) + lam_init
            y = _norm_mod_matmul(x, norm_mix[l], sc1, sh1, w_in_odd[i], tm=tm, name="in_odd", **tok)
            q, k, v = y[:, 0:d], y[:, d:2 * d], y[:, 2 * d:3 * d]
            new_k.append(k[:n_prompt].reshape(bp, seq, C_HEADS, 2, C_HD))
            new_v.append(v[:n_prompt].reshape(bp, seq, C_HEADS, 2 * C_HD))
            o_p = _diff_attention(lam, (q[:n_prompt] * scale).astype(BF16),
                                  k[:n_prompt].reshape(bp, seq, d).astype(BF16),
                                  v[:n_prompt].reshape(bp, seq, d).astype(BF16),
                                  diff_subln[i], lam_init, batch=bp, tq_total=seq, tq=min(seq, TOKEN_TILE))
            shp = (bs, dec_seq, d)
            qs = _apply_rope(q[n_prompt:].reshape(shp), cos, sin).reshape(n_sample, d)
            ks = _apply_rope(k[n_prompt:].reshape(shp), cos, sin)
            k_all = jnp.concatenate([ks, cache_k[:, i].reshape(bs, -1, d)], axis=1).astype(BF16)
            v_all = jnp.concatenate([v[n_prompt:].reshape(bs, dec_seq, d), cache_v[:, i].reshape(bs, -1, d)],
                                    axis=1).astype(BF16)
            o_s = _diff_attention(lam, (qs * scale).astype(BF16), k_all, v_all, diff_subln[i], lam_init,
                                  batch=bs, tq_total=dec_seq, tq=min(dec_seq, TOKEN_TILE))
            x = _proj_residual(x, jnp.concatenate([o_p, o_s], axis=0), g1, w_out_odd[i], tm=tm,
                               name="out_odd", **tok)

        hbt, cnt_, c1_, r2_, e2_ = _peer_select(x, norm_ffn[l], sc2, sh2, peer_w_q[l], peer_sub_keys[l],
                                                tm=tm, **tok)
        x = _peer_dense(x, hbt, g2, peer_u[l], peer_v[l], cnt_, c1_, r2_, e2_, tm=tm_dense, stages=4, **tok)

    yfin = _final_norm(x, final_norm, tm=tm)
    y_prompt = yfin[:n_prompt].reshape(bp, seq, d)
    y_sample = yfin[n_prompt:].reshape(bs, dec_seq, d)
    return (y_prompt, y_sample, jnp.stack(new_s, axis=1), jnp.stack(new_k, axis=1), jnp.stack(new_v, axis=1))
```

```python
import functools
import math

import jax
import jax.numpy as jnp
from jax import lax
from jax.experimental import pallas as pl
from jax.experimental.pallas import tpu as pltpu

D_MODEL = 1024
GRID_W = 64
ROPE_BASE = 10000.0
NORM_EPS = 1e-6
D_HALF = D_MODEL // 2
A_HEAD = 64
A_HEADS = D_HALF // A_HEAD
A_DECAY_LORA = 64
A_ICLR_LORA = 64
A_GATE_LORA = 128
GN_EPS = 64e-5
C_HEADS = 8
C_HD = D_MODEL // C_HEADS // 2
P_HEADS = 8
P_NKEYS = 128
P_EXPERTS = P_NKEYS * P_NKEYS
P_QDIM = 256
P_TOPK = 16
LORA_W = A_DECAY_LORA + A_ICLR_LORA + A_GATE_LORA
EVEN_IN = 6 * D_HALF + LORA_W

LANES = 128
SUBLANES = 8
VMEM_LIMIT = 56 * 1024 * 1024

F32 = jnp.float32
BF16 = jnp.bfloat16
NN = (((1,), (0,)), ((), ()))
NT = (((1,), (1,)), ((), ()))
NEG_INF = float("-inf")


def _cparams(*sem):
    return pltpu.CompilerParams(dimension_semantics=sem, vmem_limit_bytes=VMEM_LIMIT)


def _split2(a):
    hi = a.astype(BF16)
    lo = (a - hi.astype(F32)).astype(BF16)
    return hi, lo


def _split3(a):
    hi = a.astype(BF16)
    r = a - hi.astype(F32)
    mid = r.astype(BF16)
    lo = (r - mid.astype(F32)).astype(BF16)
    return hi, mid, lo


def _dot(a, b, dims=NN):
    return lax.dot_general(a, b, dims, preferred_element_type=F32)


def _dot3(a, b_hi, b_lo, dims=NN):
    a_hi, a_lo = _split2(a)
    return _dot(a_hi, b_hi, dims) + (_dot(a_hi, b_lo, dims) + _dot(a_lo, b_hi, dims))


def _dot_exact01(a, ones_bf16):
    hi, mid, lo = _split3(a)
    return _dot(hi, ones_bf16) + (_dot(mid, ones_bf16) + _dot(lo, ones_bf16))


def _mod_row(i, tm, n_prompt, dec_seq):
    np_tiles = n_prompt // tm
    per = dec_seq // tm
    return jnp.where(i < np_tiles, 0, 1 + (i - np_tiles) // per)


def _ada_kernel(c_ref, whi_ref, wlo_ref, b_ref, o_ref):
    c = c_ref[...]
    s = c * jax.nn.sigmoid(c)
    o_ref[0] = _dot3(s, whi_ref[0], wlo_ref[0]) + b_ref[0]


def _adaln_all(cond, w_ada, b_ada):
    depth, d, n6 = w_ada.shape
    rows = cond.shape[0]
    tn = n6 // 4
    whi, wlo = _split2(w_ada)
    return pl.pallas_call(
        _ada_kernel,
        grid=(depth, n6 // tn),
        in_specs=[
            pl.BlockSpec((rows, d), lambda l, j: (0, 0)),
            pl.BlockSpec((1, d, tn), lambda l, j: (l, 0, j)),
            pl.BlockSpec((1, d, tn), lambda l, j: (l, 0, j)),
            pl.BlockSpec((1, 1, tn), lambda l, j: (l, 0, j)),
        ],
        out_specs=pl.BlockSpec((1, rows, tn), lambda l, j: (l, 0, j)),
        out_shape=jax.ShapeDtypeStruct((depth, rows, n6), F32),
        compiler_params=_cparams("parallel", "parallel"),
        name="adaln",
    )(cond, whi, wlo, b_ada.reshape(depth, 1, n6))


def _nmm_kernel(x_ref, g_ref, sc_ref, sh_ref, w_ref, o_ref):
    x = x_ref[...]
    y = x * lax.rsqrt(jnp.mean(x * x, axis=-1, keepdims=True) + NORM_EPS)
    h = y * g_ref[...] * (1.0 + sc_ref[...]) + sh_ref[...]
    o_ref[...] = _dot(h.astype(BF16), w_ref[...])


def _norm_mod_matmul(x, gnorm, scale, shift, w, *, tm, n_prompt, dec_seq, name):
    n, d = x.shape
    nout = w.shape[1]
    tn = nout
    mod_map = lambda j, i: (_mod_row(i, tm, n_prompt, dec_seq), 0, 0)
    return pl.pallas_call(
        _nmm_kernel,
        grid=(nout // tn, n // tm),
        in_specs=[
            pl.BlockSpec((tm, d), lambda j, i: (i, 0)),
            pl.BlockSpec((1, d), lambda j, i: (0, 0)),
            pl.BlockSpec((None, 1, d), mod_map),
            pl.BlockSpec((None, 1, d), mod_map),
            pl.BlockSpec((d, tn), lambda j, i: (0, j)),
        ],
        out_specs=pl.BlockSpec((tm, tn), lambda j, i: (i, j)),
        out_shape=jax.ShapeDtypeStruct((n, nout), F32),
        compiler_params=_cparams("arbitrary", "arbitrary"),
        name=name,
    )(x, gnorm.reshape(1, d), scale, shift, w.astype(BF16))


def _proj_res_kernel(x_ref, a_ref, gate_ref, w_ref, o_ref):
    o_ref[...] = x_ref[...] + gate_ref[...] * _dot(a_ref[...].astype(BF16), w_ref[...])


def _proj_residual(x, a, gate, w, *, tm, n_prompt, dec_seq, name):
    n, d = x.shape
    k = a.shape[1]
    mod_map = lambda i: (_mod_row(i, tm, n_prompt, dec_seq), 0, 0)
    return pl.pallas_call(
        _proj_res_kernel,
        grid=(n // tm,),
        in_specs=[
            pl.BlockSpec((tm, d), lambda i: (i, 0)),
            pl.BlockSpec((tm, k), lambda i: (i, 0)),
            pl.BlockSpec((None, 1, d), mod_map),
            pl.BlockSpec((k, d), lambda i: (0, 0)),
        ],
        out_specs=pl.BlockSpec((tm, d), lambda i: (i, 0)),
        out_shape=jax.ShapeDtypeStruct((n, d), F32),
        compiler_params=_cparams("parallel"),
        name=name,
    )(x, a, gate, w.astype(BF16))


def _rwkv_prep_kernel(r_ref, k_ref, v_ref, cg_ref, hc_ref, lora_ref,
                      w0_ref, wuph_ref, wupl_ref, a0_ref, auph_ref, aupl_ref, guph_ref, gupl_ref,
                      kk_ref_p, ka_ref_p, rk_ref_p, ones_ref,
                      r_out, v_out, kk_out, w_out, kd_out, kka_out, bonus_out, g_out, z_out):
    r = r_ref[...]
    k = k_ref[...]
    v = v_ref[...]
    r_out[...] = r
    v_out[...] = v
    lora = lora_ref[...]
    ones = ones_ref[...]
    wd = lora[:, :A_DECAY_LORA]
    ad = lora[:, A_DECAY_LORA:A_DECAY_LORA + A_ICLR_LORA]
    gd = lora[:, A_DECAY_LORA + A_ICLR_LORA:]

    kk = k * kk_ref_p[...]
    nrm = jnp.sqrt(_dot_exact01(kk * kk, ones))
    kk = kk / jnp.maximum(nrm, 1e-12)
    kk_out[...] = kk

    tw = jnp.tanh(wd)
    bonus = None
    for d in range(2):
        w_pre = w0_ref[d] + _dot3(tw, wuph_ref[d], wupl_ref[d])
        w_out[d] = jnp.exp(-(math.exp(-0.5) * jax.nn.sigmoid(w_pre)))
        a = jax.nn.sigmoid(a0_ref[d] + _dot3(ad, auph_ref[d], aupl_ref[d]))
        kd = k * (1.0 + (a - 1.0) * ka_ref_p[...])
        kd_out[d] = kd
        kka_out[d] = kk * a
        b = _dot_exact01(r * kd * rk_ref_p[...], ones) * v
        bonus = b if bonus is None else bonus + b
    bonus_out[...] = bonus
    g_out[...] = _dot3(jax.nn.sigmoid(gd), guph_ref[...], gupl_ref[...])
    z_out[...] = cg_ref[...] * hc_ref[...]


def _head_ones():
    idx = jnp.arange(D_HALF) // A_HEAD
    return (idx[:, None] == idx[None, :]).astype(BF16)


def _rwkv_prep(y, w0, w_up, a0, a_up, g_up, k_k, k_a, r_k, *, tm, row0, n):
    off = row0 // tm
    h = D_HALF
    wuph, wupl = _split2(w_up)
    auph, aupl = _split2(a_up)
    guph, gupl = _split2(g_up)
    col = lambda c: pl.BlockSpec((tm, h), lambda i: (i + off, c))
    full2 = lambda s: pl.BlockSpec(s, lambda i: (0, 0))
    full3 = lambda s: pl.BlockSpec(s, lambda i: (0, 0, 0))
    tok = pl.BlockSpec((tm, h), lambda i: (i, 0))
    tok2 = pl.BlockSpec((2, tm, h), lambda i: (0, i, 0))
    sds = lambda *s: jax.ShapeDtypeStruct(s, F32)
    return pl.pallas_call(
        _rwkv_prep_kernel,
        grid=(n // tm,),
        in_specs=[col(0), col(1), col(2), col(4), col(5),
                  pl.BlockSpec((tm, LORA_W), lambda i: (i + off, 6 * h // LORA_W)),
                  full3((2, 1, h)), full3((2, A_DECAY_LORA, h)), full3((2, A_DECAY_LORA, h)),
                  full3((2, 1, h)), full3((2, A_ICLR_LORA, h)), full3((2, A_ICLR_LORA, h)),
                  full2((A_GATE_LORA, h)), full2((A_GATE_LORA, h)),
                  full2((1, h)), full2((1, h)), full2((1, h)), full2((h, h))],
        out_specs=[tok, tok, tok, tok2, tok2, tok2, tok, tok, tok],
        out_shape=[sds(n, h), sds(n, h), sds(n, h), sds(2, n, h), sds(2, n, h), sds(2, n, h),
                   sds(n, h), sds(n, h), sds(n, h)],
        compiler_params=_cparams("parallel"),
        name="rwkv_prep",
    )(y, y, y, y, y, y,
      w0.reshape(2, 1, h), wuph, wupl, a0.reshape(2, 1, h), auph, aupl, guph, gupl,
      k_k.reshape(1, h), k_a.reshape(1, h), r_k.reshape(1, h), _head_ones())


SCAN_ROW_BLOCK = 32
SCAN_CHUNK = 16


def _scan_kernel(r_ref, kk_ref, v_ref, w_ref, kd_ref, kka_ref, s0_ref, o_ref, sf_ref, s_scr, *, tc, g):
    d = pl.program_id(0)
    c = pl.program_id(2)
    nk = A_HEAD
    parts = 4

    @pl.when(c == 0)
    def _():
        s_scr[...] = s0_ref[...]

    gb = min(g, SCAN_ROW_BLOCK)

    def row(ref, t, k):
        return jnp.broadcast_to(ref[t, pl.ds(k, 1), :], (gb, LANES))

    def tree(xs):
        while len(xs) > 1:
            xs = [xs[i] + xs[i + 1] for i in range(0, len(xs) - 1, 2)] + ([xs[-1]] if len(xs) % 2 else [])
        return xs[0]

    def step(i, carry):
        t = i + d * (tc - 1 - 2 * i)
        for g0 in range(0, g, gb):
            rows = slice(g0, g0 + gb)
            acc = [None] * parts
            for k in range(nk):
                p = s_scr[k, rows, :] * row(kk_ref, t, k)
                acc[k % parts] = p if acc[k % parts] is None else acc[k % parts] + p
            sa = tree(acc)
            vv = v_ref[t, rows, :]
            acc = [None] * parts
            for k in range(nk):
                sn = s_scr[k, rows, :] * row(w_ref, t, k) - sa * row(kka_ref, t, k) + vv * row(kd_ref, t, k)
                s_scr[k, rows, :] = sn
                p = sn * row(r_ref, t, k)
                acc[k % parts] = p if acc[k % parts] is None else acc[k % parts] + p
            o_ref[t, rows, :] = tree(acc)
        return carry

    lax.fori_loop(0, tc, step, 0)

    @pl.when(c == pl.num_programs(2) - 1)
    def _():
        sf_ref[...] = s_scr[...]


def _wkv_scan(r, kk, v, w, kd, kka, s0, *, tc):
    lg, t, nk, _ = r.shape
    g = v.shape[2]
    nt = t // tc
    tmap = lambda d, l, c: c + d * (nt - 1 - 2 * c)
    shared = lambda rows: pl.BlockSpec((None, tc, rows, LANES), lambda d, l, c: (l, tmap(d, l, c), 0, 0))
    perdir = pl.BlockSpec((None, None, tc, nk, LANES), lambda d, l, c: (d, l, tmap(d, l, c), 0, 0))
    state = pl.BlockSpec((None, None, nk, g, LANES), lambda d, l, c: (d, l, 0, 0, 0))
    return pl.pallas_call(
        functools.partial(_scan_kernel, tc=tc, g=g),
        grid=(2, lg, nt),
        in_specs=[shared(nk), shared(nk), shared(g), perdir, perdir, perdir, state],
        out_specs=[pl.BlockSpec((None, None, tc, g, LANES), lambda d, l, c: (d, l, tmap(d, l, c), 0, 0)), state],
        out_shape=[jax.ShapeDtypeStruct((2, lg, t, g, LANES), F32),
                   jax.ShapeDtypeStruct((2, lg, nk, g, LANES), F32)],
        scratch_shapes=[pltpu.VMEM((nk, g, LANES), F32)],
        compiler_params=_cparams("parallel", "parallel", "arbitrary"),
        name="wkv_scan",
    )(r, kk, v, w, kd, kka, s0)


def _chain_layout(batch):
    chains = batch * A_HEADS
    if chains >= LANES:
        assert chains % LANES == 0
        return chains // LANES, 1
    assert LANES % chains == 0 and A_HEAD % (LANES // chains) == 0
    return 1, LANES // chains


def _to_lanes_k(x, batch, t, lg, vq):
    lead = x.shape[:-2]
    nl = len(lead)
    chains = (batch // lg) * A_HEADS
    x = x.reshape(lead + (lg, 1, batch // lg, t, A_HEADS, A_HEAD))
    x = jnp.broadcast_to(x, lead + (lg, vq, batch // lg, t, A_HEADS, A_HEAD))
    perm = tuple(range(nl)) + tuple(nl + p for p in (0, 3, 5, 1, 2, 4))
    return jnp.transpose(x, perm).reshape(lead + (lg, t, A_HEAD, vq * chains))


def _to_lanes_v(x, batch, t, lg, vq):
    gsz = A_HEAD // vq
    x = x.reshape(lg, batch // lg, t, A_HEADS, vq, gsz)
    x = jnp.transpose(x, (0, 2, 5, 4, 1, 3))
    return x.reshape(lg, t, gsz, vq * (batch // lg) * A_HEADS)


def _from_lanes_v(o, batch, t, lg, vq):
    lead = o.shape[:-4]
    gsz = A_HEAD // vq
    o = o.reshape(lead + (lg, t, gsz, vq, batch // lg, A_HEADS))
    nl = len(lead)
    perm = tuple(range(nl)) + tuple(nl + p for p in (0, 4, 1, 5, 3, 2))
    return jnp.transpose(o, perm).reshape(lead + (batch * t, D_HALF))


def _state_to_lanes(s, lg, vq):
    batch = s.shape[0]
    gsz = A_HEAD // vq
    s = s.reshape(lg, batch // lg, 2, A_HEADS, vq, gsz, A_HEAD)
    s = jnp.transpose(s, (2, 0, 6, 5, 4, 1, 3))
    return s.reshape(2, lg, A_HEAD, gsz, LANES)


def _state_from_lanes(s, batch, lg, vq):
    gsz = A_HEAD // vq
    s = s.reshape(2, lg, A_HEAD, gsz, vq, batch // lg, A_HEADS)
    s = jnp.transpose(s, (1, 5, 0, 6, 4, 3, 2))
    return s.reshape(batch, 2, A_HEADS, A_HEAD, A_HEAD)


def _even_post_kernel(x_ref, o_ref_in, bonus_ref, g_ref, bg_ref, z_ref, zprev_ref, znext_ref,
                      lng_ref, lnb_ref, cw_ref, ones_ref, gate_ref, w_ref, out_ref,
                      *, tm, np_tiles, seq_tiles, dec_tiles):
    i = pl.program_id(0)
    ones = ones_ref[...]
    inv = 1.0 / A_HEAD
    o = o_ref_in[0] + o_ref_in[1]
    mu = _dot_exact01(o, ones) * inv
    xc = o - mu
    var = _dot_exact01(xc * xc, ones) * inv
    on = xc * lax.rsqrt(var + GN_EPS) * lng_ref[...] + lnb_ref[...]
    ya = (on + bonus_ref[...]) * g_ref[...]

    pos = jnp.where(i < np_tiles, i % seq_tiles, (i - np_tiles) % dec_tiles)
    last = jnp.where(i < np_tiles, seq_tiles - 1, dec_tiles - 1)
    z = z_ref[...]
    rows = lax.broadcasted_iota(jnp.int32, z.shape, 0)
    prev_row = jnp.where(pos == 0, 0.0, zprev_ref[pl.ds(SUBLANES - 1, 1), :])
    next_row = jnp.where(pos == last, 0.0, znext_ref[pl.ds(0, 1), :])
    zp = jnp.where(rows == 0, prev_row, pltpu.roll(z, 1, axis=0))
    zn = jnp.where(rows == tm - 1, next_row, pltpu.roll(z, tm - 1, axis=0))
    yb = bg_ref[...] * (zp * cw_ref[pl.ds(0, 1), :] + z * cw_ref[pl.ds(1, 1), :] + zn * cw_ref[pl.ds(2, 1), :])

    h = D_HALF
    y = _dot(ya.astype(BF16), w_ref[pl.ds(0, h), :]) + _dot(yb.astype(BF16), w_ref[pl.ds(h, h), :])
    out_ref[...] = x_ref[...] + gate_ref[...] * y


def _even_post(x, o2, bonus, g, y, z, ln_g, ln_b, conv_w, gate, w_out, *, tm, n_prompt, seq, dec_seq):
    n, d = x.shape
    h = D_HALF
    nblk8 = n // SUBLANES
    per8 = tm // SUBLANES
    tok = pl.BlockSpec((tm, h), lambda i: (i, 0))
    full2 = lambda s: pl.BlockSpec(s, lambda i: (0, 0))
    mod_map = lambda i: (_mod_row(i, tm, n_prompt, dec_seq), 0, 0)
    return pl.pallas_call(
        functools.partial(_even_post_kernel, tm=tm, np_tiles=n_prompt // tm, seq_tiles=seq // tm,
                          dec_tiles=dec_seq // tm),
        grid=(n // tm,),
        in_specs=[
            pl.BlockSpec((tm, d), lambda i: (i, 0)),
            pl.BlockSpec((2, tm, h), lambda i: (0, i, 0)),
            tok, tok,
            pl.BlockSpec((tm, h), lambda i: (i, 3)),
            tok,
            pl.BlockSpec((SUBLANES, h), lambda i: (jnp.maximum(i * per8 - 1, 0), 0)),
            pl.BlockSpec((SUBLANES, h), lambda i: (jnp.minimum((i + 1) * per8, nblk8 - 1), 0)),
            full2((1, h)), full2((1, h)), full2((SUBLANES, h)), full2((h, h)),
            pl.BlockSpec((None, 1, d), mod_map),
            full2((d, d)),
        ],
        out_specs=pl.BlockSpec((tm, d), lambda i: (i, 0)),
        out_shape=jax.ShapeDtypeStruct((n, d), F32),
        compiler_params=_cparams("parallel"),
        name="even_post",
    )(x, o2, bonus, g, y, z, z, z, ln_g.reshape(1, h), ln_b.reshape(1, h),
      jnp.pad(conv_w, ((0, SUBLANES - conv_w.shape[0]), (0, 0))), _head_ones(), gate, w_out.astype(BF16))


def _attn_kernel(lam_ref, q_ref, k_ref, v_ref, subln_ref, o_ref, *, out_scale):
    lam = lam_ref[0]
    hd2 = 2 * C_HD
    for h in range(C_HEADS):
        parts = []
        for j in range(2):
            lo = h * hd2 + j * C_HD
            s = _dot(q_ref[:, lo:lo + C_HD], k_ref[:, lo:lo + C_HD], NT)
            e = jnp.exp(s - jnp.max(s, axis=-1, keepdims=True))
            inv = 1.0 / jnp.sum(e, axis=-1, keepdims=True)
            parts.append(_dot(e.astype(BF16), v_ref[:, h * hd2:(h + 1) * hd2]) * inv)
        oh = parts[0] - lam * parts[1]
        y = oh * lax.rsqrt(jnp.mean(oh * oh, axis=-1, keepdims=True) + NORM_EPS)
        o_ref[:, h * hd2:(h + 1) * hd2] = y * subln_ref[...] * out_scale


def _diff_attention(lam, q, k, v, subln, lam_init, *, batch, tq_total, tq):
    d = q.shape[1]
    tk = k.shape[1]
    nq = tq_total // tq
    return pl.pallas_call(
        functools.partial(_attn_kernel, out_scale=1.0 - lam_init),
        grid=(batch, nq),
        in_specs=[
            pl.BlockSpec(memory_space=pltpu.SMEM),
            pl.BlockSpec((tq, d), lambda b, i: (b * nq + i, 0)),
            pl.BlockSpec((None, tk, d), lambda b, i: (b, 0, 0)),
            pl.BlockSpec((None, tk, d), lambda b, i: (b, 0, 0)),
            pl.BlockSpec((1, 2 * C_HD), lambda b, i: (0, 0)),
        ],
        out_specs=pl.BlockSpec((tq, d), lambda b, i: (b * nq + i, 0)),
        out_shape=jax.ShapeDtypeStruct((batch * tq_total, d), F32),
        compiler_params=_cparams("parallel", "parallel"),
        name="diff_attn",
    )(lam.reshape(1), q, k, v, subln.reshape(1, 2 * C_HD))


def _axial_rope_tables(n_tok):
    rows = n_tok // GRID_W
    row = jnp.repeat(jnp.arange(rows, dtype=F32), GRID_W)
    col = jnp.tile(jnp.arange(GRID_W, dtype=F32), rows)
    nf = C_HD // 4
    inv = ROPE_BASE ** (-jnp.arange(nf, dtype=F32) / nf)
    ar = row[:, None] * inv[None, :]
    ac = col[:, None] * inv[None, :]
    ang = jnp.concatenate([ar, ar, ac, ac], axis=-1)
    return jnp.cos(ang), jnp.sin(ang)


def _apply_rope(x, cos, sin):
    q = C_HD // 4
    first = (jnp.arange(x.shape[-1]) % (2 * q)) < q
    xr = jnp.where(first, -jnp.roll(x, -q, axis=-1), jnp.roll(x, q, axis=-1))
    return x * cos[None] + xr * sin[None]


N_TOP = P_TOPK + 1
CAND_PAIRS = [(i, j) for i in range(N_TOP) for j in range(N_TOP) if (i + 1) * (j + 1) <= N_TOP]
CAND_ROWS = -(-len(CAND_PAIRS) // SUBLANES) * SUBLANES
TOP_ROWS = -(-N_TOP // SUBLANES) * SUBLANES


def _take_top(x, n, emit, want_rank=False):
    rank = jnp.full(x.shape, float(n), F32) if want_rank else None
    for it in range(n):
        m = jnp.max(x, axis=0, keepdims=True)
        emit(it, m)
        hit = x == m
        if want_rank:
            rank = jnp.where(hit, float(it), rank)
        if it + 1 < n:
            x = jnp.where(hit, NEG_INF, x)
    return rank


def _pack_bf16_rows(x):
    return pltpu.bitcast(x.astype(BF16), jnp.int32)


def _peer_select_kernel(x_ref, g_ref, sc_ref, sh_ref, wq_ref, skh_ref, skl_ref,
                        hbt_out, l_out, c1_out, r2_out, e2_out, top_scr, cand_scr):
    x = x_ref[...]
    y = x * lax.rsqrt(jnp.mean(x * x, axis=-1, keepdims=True) + NORM_EPS)
    hmod = y * g_ref[...] * (1.0 + sc_ref[...]) + sh_ref[...]
    hbt_out[...] = hmod.T.astype(BF16)
    q = _dot(hmod.astype(BF16), wq_ref[...])
    half = P_QDIM // 2
    for h in range(P_HEADS):
        sT = []
        for j in range(2):
            hj = 2 * h + j
            qs = q[:, hj * half:(hj + 1) * half]
            q_hi, q_lo = _split2(qs)
            s = _dot(skh_ref[hj], q_hi, NT) + (_dot(skh_ref[hj], q_lo, NT) + _dot(skl_ref[hj], q_hi, NT))
            sT.append(s)

            def emit(it, m, j=j):
                top_scr[j, pl.ds(it, 1), :] = m
            rank = _take_top(s, N_TOP, emit, want_rank=(j == 1))

        cand_scr[...] = jnp.full(cand_scr.shape, NEG_INF, F32)
        for c, (i, j) in enumerate(CAND_PAIRS):
            cand_scr[pl.ds(c, 1), :] = top_scr[0, pl.ds(i, 1), :] + top_scr[1, pl.ds(j, 1), :]
        a1 = top_scr[0, pl.ds(0, 1), :]
        b1 = top_scr[1, pl.ds(0, 1), :]
        mx = a1 + b1
        st = {"z": None, "c16": None, "c17": None}

        def emit_c(it, m):
            if it < P_TOPK:
                e = jnp.exp(m - mx)
                st["z"] = e if st["z"] is None else st["z"] + e
            if it == P_TOPK - 1:
                st["c16"] = m
            if it == P_TOPK:
                st["c17"] = m
        _take_top(cand_scr[...], N_TOP, emit_c)
        tau = 0.5 * (st["c16"] + st["c17"])
        thr = tau - sT[0]
        cnt = jnp.zeros(thr.shape, F32)
        for k in range(N_TOP):
            cnt = cnt + jnp.where(top_scr[1, pl.ds(k, 1), :] >= thr, 1.0, 0.0)
        l_out[h] = cnt
        c1_out[h] = jnp.exp(sT[0] - a1) / st["z"]
        e2 = jnp.exp(sT[1] - b1)
        for g in range(e2.shape[1] // LANES):
            r2_out[h, g] = _pack_bf16_rows(rank[:, g * LANES:(g + 1) * LANES])
            e2_out[h, g] = _pack_bf16_rows(e2[:, g * LANES:(g + 1) * LANES])


def _peer_select(x, gnorm, scale, shift, w_q, sub_keys, *, tm, n_prompt, dec_seq):
    n, d = x.shape
    sk = sub_keys.reshape(2 * P_HEADS, P_NKEYS, P_QDIM // 2)
    skh, skl = _split2(sk)
    mod_map = lambda i: (_mod_row(i, tm, n_prompt, dec_seq), 0, 0)
    out = pl.BlockSpec((P_HEADS, P_NKEYS, tm), lambda i: (0, 0, i))
    sds = jax.ShapeDtypeStruct((P_HEADS, P_NKEYS, n), F32)
    out_g = pl.BlockSpec((P_HEADS, tm // LANES, P_NKEYS // 2, LANES), lambda i: (0, i, 0, 0))
    sds_g = jax.ShapeDtypeStruct((P_HEADS, n // LANES, P_NKEYS // 2, LANES), jnp.int32)
    return pl.pallas_call(
        _peer_select_kernel,
        grid=(n // tm,),
        in_specs=[pl.BlockSpec((tm, d), lambda i: (i, 0)),
                  pl.BlockSpec((1, d), lambda i: (0, 0)),
                  pl.BlockSpec((None, 1, d), mod_map),
                  pl.BlockSpec((None, 1, d), mod_map),
                  pl.BlockSpec(w_q.shape, lambda i: (0, 0)),
                  pl.BlockSpec(sk.shape, lambda i: (0, 0, 0)),
                  pl.BlockSpec(sk.shape, lambda i: (0, 0, 0))],
        out_specs=[pl.BlockSpec((d, tm), lambda i: (0, i)), out, out, out_g, out_g],
        out_shape=[jax.ShapeDtypeStruct((d, n), BF16), sds, sds, sds_g, sds_g],
        scratch_shapes=[pltpu.VMEM((2, TOP_ROWS, tm), F32), pltpu.VMEM((CAND_ROWS, tm), F32)],
        compiler_params=_cparams("parallel"),
        name="peer_select",
    )(x, gnorm.reshape(1, d), scale, shift, w_q.astype(BF16), skh, skl)


PAIR = 2
PAIR_ROWS = PAIR * P_NKEYS
I2_HALF = P_NKEYS // 2


def _gelu(x):
    return 0.5 * x * (1.0 + lax.erf(x * (1.0 / math.sqrt(2.0))))


def _peer_dense_kernel(hbt_ref, x_ref, gate_ref, ufirst_ref, unext_ref, vtprev_ref, vtlast_ref,
                       l_ref, c1_ref, r2_ref, e2_ref, o_ref, acc_ref, a_even, a_odd, act_even, act_odd,
                       *, stages, tm):
    c = pl.program_id(1)

    def put_act(act_ref, act):
        for q in range(stages):
            for lg in range(tm // LANES):
                act_ref[q, lg] = act[q * PAIR_ROWS:(q + 1) * PAIR_ROWS, lg * LANES:(lg + 1) * LANES]

    @pl.when(c == 0)
    def _():
        acc_ref[...] = jnp.zeros(acc_ref.shape, F32)
        a_odd[...] = jnp.zeros(a_odd.shape, BF16)
        put_act(act_even, _gelu(_dot(ufirst_ref[...], hbt_ref[...])))

    def chunk(act_cur, act_next, a_cur, a_prev):
        put_act(act_next, _gelu(_dot(unext_ref[...], hbt_ref[...])))
        for p in range(stages):
            for lg in range(tm // LANES):
                ls = slice(lg * LANES, (lg + 1) * LANES)
                for half in range(P_NKEYS // I2_HALF):
                    words = slice(half * I2_HALF // 2, (half + 1) * I2_HALF // 2)
                    w = [None] * PAIR
                    for h in range(P_HEADS):
                        r2 = pltpu.bitcast(r2_ref[h, lg, words, :], BF16)
                        e2 = pltpu.bitcast(e2_ref[h, lg, words, :], BF16)
                        for j in range(PAIR):
                            row = pl.ds(PAIR * p + j, 1)
                            cnt = jnp.broadcast_to(l_ref[h, row, ls], (I2_HALF, LANES)).astype(BF16)
                            c1 = jnp.broadcast_to(c1_ref[h, row, ls], (I2_HALF, LANES)).astype(BF16)
                            term = jnp.where(r2 < cnt, e2 * c1, jnp.zeros_like(e2))
                            w[j] = term if w[j] is None else w[j] + term
                    for j in range(PAIR):
                        off = j * P_NKEYS + half * I2_HALF
                        rs = slice(p * PAIR_ROWS + off, p * PAIR_ROWS + off + I2_HALF)
                        a_cur[rs, ls] = w[j] * act_cur[p, lg, off:off + I2_HALF, :].astype(BF16)
        acc_ref[...] += _dot(vtprev_ref[...], a_prev[...])

    @pl.when(c % 2 == 0)
    def _():
        chunk(act_even, act_odd, a_even, a_odd)

    @pl.when(c % 2 == 1)
    def _():
        chunk(act_odd, act_even, a_odd, a_even)

    @pl.when(c == pl.num_programs(1) - 1)
    def _():
        total = acc_ref[...] + _dot(vtlast_ref[...], a_odd[...])
        o_ref[...] = x_ref[...] + gate_ref[...] * total.T


def _peer_dense(x, hbt, gate, u, v, cnt, c1, r2, e2, *, tm, stages, n_prompt, dec_seq):
    n, d = x.shape
    ec = stages * PAIR_ROWS
    n_chunks = P_EXPERTS // ec
    assert n_chunks % 2 == 0
    u_bf = u.astype(BF16)
    vt_bf = v.astype(BF16).T
    mod_map = lambda i, c: (_mod_row(i, tm, n_prompt, dec_seq), 0, 0)
    small = pl.BlockSpec((P_HEADS, stages * PAIR, tm), lambda i, c: (0, c, i))
    big = pl.BlockSpec((P_HEADS, tm // LANES, P_NKEYS // 2, LANES), lambda i, c: (0, i, 0, 0))
    act_shape = (stages, tm // LANES, PAIR_ROWS, LANES)
    return pl.pallas_call(
        functools.partial(_peer_dense_kernel, stages=stages, tm=tm),
        grid=(n // tm, n_chunks),
        in_specs=[
            pl.BlockSpec((d, tm), lambda i, c: (0, i)),
            pl.BlockSpec((tm, d), lambda i, c: (i, 0)),
            pl.BlockSpec((None, 1, d), mod_map),
            pl.BlockSpec((ec, d), lambda i, c: (0, 0)),
            pl.BlockSpec((ec, d), lambda i, c: (jnp.minimum(c + 1, n_chunks - 1), 0)),
            pl.BlockSpec((d, ec), lambda i, c: (0, jnp.maximum(c - 1, 0))),
            pl.BlockSpec((d, ec), lambda i, c: (0, n_chunks - 1)),
            small, small, big, big,
        ],
        out_specs=pl.BlockSpec((tm, d), lambda i, c: (i, 0)),
        out_shape=jax.ShapeDtypeStruct((n, d), F32),
        scratch_shapes=[pltpu.VMEM((d, tm), F32), pltpu.VMEM((ec, tm), BF16), pltpu.VMEM((ec, tm), BF16),
                        pltpu.VMEM(act_shape, F32), pltpu.VMEM(act_shape, F32)],
        compiler_params=_cparams("parallel", "arbitrary"),
        name="peer_dense",
    )(hbt, x, gate, u_bf, u_bf, vt_bf, vt_bf, cnt, c1, r2, e2)


def _final_norm_kernel(x_ref, g_ref, o_ref):
    x = x_ref[...]
    o_ref[...] = x * lax.rsqrt(jnp.mean(x * x, axis=-1, keepdims=True) + NORM_EPS) * g_ref[...]


def _final_norm(x, g, *, tm):
    n, d = x.shape
    return pl.pallas_call(
        _final_norm_kernel,
        grid=(n // tm,),
        in_specs=[pl.BlockSpec((tm, d), lambda i: (i, 0)), pl.BlockSpec((1, d), lambda i: (0, 0))],
        out_specs=pl.BlockSpec((tm, d), lambda i: (i, 0)),
        out_shape=jax.ShapeDtypeStruct((n, d), F32),
        compiler_params=_cparams("parallel"),
        name="final_norm",
    )(x, g.reshape(1, d))


TOKEN_TILE = 256
DENSE_TILE = 512


def _tiles(n_prompt, seq, dec_seq):
    tm = math.gcd(math.gcd(seq, dec_seq), TOKEN_TILE)
    tm_dense = math.gcd(math.gcd(n_prompt, dec_seq), DENSE_TILE)
    return tm, tm_dense


def kernel(x_prompt, x_sample, c, state_rwkv, cache_k, cache_v, c_ctx, w_ada, b_ada, norm_mix, norm_ffn, w_in_even, rwkv_w0, rwkv_w_up, rwkv_a0, rwkv_a_up, rwkv_g_up, rwkv_k_k, rwkv_k_a, rwkv_r_k, rwkv_ln_g, rwkv_ln_b, conv_w, w_out_even, w_in_odd, diff_lam, diff_subln, w_out_odd, peer_w_q, peer_sub_keys, peer_u, peer_v, final_norm):
    bp, seq, d = x_prompt.shape
    bs, dec_seq, _ = x_sample.shape
    depth = w_ada.shape[0]
    n_prompt = bp * seq
    n_sample = bs * dec_seq
    tm, tm_dense = _tiles(n_prompt, seq, dec_seq)
    tok = dict(n_prompt=n_prompt, dec_seq=dec_seq)

    x = jnp.concatenate([x_prompt.reshape(n_prompt, d), x_sample.reshape(n_sample, d)], axis=0)

    rows = 1 + bs
    rows_pad = -(-rows // SUBLANES) * SUBLANES
    cond = jnp.zeros((rows_pad, d), F32).at[0].set(c_ctx).at[1:rows].set(c)
    mod = _adaln_all(cond, w_ada, b_ada)[:, :rows].reshape(depth, rows, 6, 1, d)
    mods = [[mod[l, :, m] for m in range(6)] for l in range(depth)]

    h = D_HALF
    perm = jnp.concatenate([jnp.arange(0, 3 * h), jnp.arange(3 * h + LORA_W, 6 * h + LORA_W),
                            jnp.arange(3 * h, 3 * h + LORA_W)])

    lg_p, vq_p = _chain_layout(bp)
    lg_s, vq_s = _chain_layout(bs)
    cos, sin = (jnp.tile(tab, (1, d // C_HD)) for tab in _axial_rope_tables(dec_seq))
    scale = C_HD ** -0.5

    new_s, new_k, new_v = [], [], []
    for l in range(depth):
        i = l // 2
        sh1, sc1, g1, sh2, sc2, g2 = mods[l]
        if l % 2 == 0:
            y = _norm_mod_matmul(x, norm_mix[l], sc1, sh1, w_in_even[i][:, perm], tm=tm, name="in_even", **tok)
            outs, tails = [], []
            for (lo, n_tok, batch, t, lg, vq, s0) in (
                    (0, n_prompt, bp, seq, lg_p, vq_p, None),
                    (n_prompt, n_sample, bs, dec_seq, lg_s, vq_s, state_rwkv[:, i])):
                r, v, kk, w, kd, kka, bonus, g, z = _rwkv_prep(
                    y, rwkv_w0[i], rwkv_w_up[i], rwkv_a0[i], rwkv_a_up[i], rwkv_g_up[i],
                    rwkv_k_k[i], rwkv_k_a[i], rwkv_r_k[i].reshape(-1), tm=tm, row0=lo, n=n_tok)
                tails.append((bonus, g, z))
                lk = lambda a: _to_lanes_k(a, batch, t, lg, vq)
                gsz = A_HEAD // vq
                s0l = (jnp.zeros((2, lg, A_HEAD, gsz, LANES), F32) if s0 is None
                       else _state_to_lanes(s0, lg, vq))
                o, sf = _wkv_scan(lk(r), lk(kk), _to_lanes_v(v, batch, t, lg, vq),
                                  lk(w), lk(kd), lk(kka), s0l, tc=min(SCAN_CHUNK, t))
                outs.append(_from_lanes_v(o, batch, t, lg, vq))
                if s0 is None:
                    new_s.append(_state_from_lanes(sf, batch, lg, vq))
            o2 = jnp.concatenate(outs, axis=1)
            bonus, g, z = (jnp.concatenate(parts, axis=0) for parts in zip(*tails))
            x = _even_post(x, o2, bonus, g, y, z, rwkv_ln_g[i], rwkv_ln_b[i], conv_w[i], g1, w_out_even[i],
                           tm=tm, n_prompt=n_prompt, seq=seq, dec_seq=dec_seq)
        else:
            lam_init = 0.8 - 0.6 * math.exp(-0.3 * l)
            lp = diff_lam[i].astype(F32)
            lam = jnp.exp(jnp.sum(lp[0] * lp[1])) - jnp.exp(jnp.sum(lp[2] * lp[3])) + lam_init
            y = _norm_mod_matmul(x, norm_mix[l], sc1, sh1, w_in_odd[i], tm=tm, name="in_odd", **tok)
            q, k, v = y[:, 0:d], y[:, d:2 * d], y[:, 2 * d:3 * d]
            new_k.append(k[:n_prompt].reshape(bp, seq, C_HEADS, 2, C_HD))
            new_v.append(v[:n_prompt].reshape(bp, seq, C_HEADS, 2 * C_HD))
            o_p = _diff_attention(lam, (q[:n_prompt] * scale).astype(BF16),
                                  k[:n_prompt].reshape(bp, seq, d).astype(BF16),
                                  v[:n_prompt].reshape(bp, seq, d).astype(BF16),
                                  diff_subln[i], lam_init, batch=bp, tq_total=seq, tq=min(seq, TOKEN_TILE))
            shp = (bs, dec_seq, d)
            qs = _apply_rope(q[n_prompt:].reshape(shp), cos, sin).reshape(n_sample, d)
            ks = _apply_rope(k[n_prompt:].reshape(shp), cos, sin)
            k_all = jnp.concatenate([ks, cache_k[:, i].reshape(bs, -1, d)], axis=1).astype(BF16)
            v_all = jnp.concatenate([v[n_prompt:].reshape(bs, dec_seq, d), cache_v[:, i].reshape(bs, -1, d)],
                                    axis=1).astype(BF16)
            o_s = _diff_attention(lam, (qs * scale).astype(BF16), k_all, v_all, diff_subln[i], lam_init,
                                  batch=bs, tq_total=dec_seq, tq=min(dec_seq, TOKEN_TILE))
            x = _proj_residual(x, jnp.concatenate([o_p, o_s], axis=0), g1, w_out_odd[i], tm=tm,
                               name="out_odd", **tok)

        hbt, cnt_, c1_, r2_, e2_ = _peer_select(x, norm_ffn[l], sc2, sh2, peer_w_q[l], peer_sub_keys[l],
                                                tm=tm, **tok)
        x = _peer_dense(x, hbt, g2, peer_u[l], peer_v[l], cnt_, c1_, r2_, e2_, tm=tm_dense, stages=4, **tok)

    yfin = _final_norm(x, final_norm, tm=tm)
    y_prompt = yfin[:n_prompt].reshape(bp, seq, d)
    y_sample = yfin[n_prompt:].reshape(bs, dec_seq, d)
    return (y_prompt, y_sample, jnp.stack(new_s, axis=1), jnp.stack(new_k, axis=1), jnp.stack(new_v, axis=1))
```
